```python
import jax
import jax.numpy as jnp
from jax import lax
import numpy as np

D_MODEL = 1024
BATCH = 8
SEQ = 4096
DEPTH = 1
DEC_BATCH = 128
DEC_SEQ = 4
PAST_LEN = 8192
PAGE_SIZE = 128

N_HEADS = 8
HEAD_DIM = 64
D_ATTN = N_HEADS * HEAD_DIM
D_CONV = D_MODEL // 2
CONV_W = 3
D_FF = 2816
FFN_CONV_W = 3
Q_BLOCK = 128
N_MOD = 6
EPS = 1e-6
SB_BIAS_LO = 6.0
SB_BIAS_HI = 9.0
D_IN = 3 * D_CONV + 3 * D_ATTN + 2 * D_MODEL

kernel_name = 'sb_shortconv_hybrid_step'


def rms_norm(x, g):
    xf = x.astype(jnp.float32)
    y = xf * lax.rsqrt(jnp.mean(xf * xf, axis=-1, keepdims=True) + EPS)
    return (y * g.astype(jnp.float32)).astype(x.dtype)


def modulate(h, shift, scale):
    return h * (1 + scale[:, None, :]) + shift[:, None, :]


def causal_dwconv(x_ext, w, b):
    width = w.shape[0]
    t = x_ext.shape[1] - (width - 1)
    y = b
    for j in range(width):
        y = y + w[j] * x_ext[:, j:j + t]
    return y


def sb_logits(q, k, bias):
    z = jnp.einsum('bqhd,bkhd->bhqk', q, k, preferred_element_type=jnp.float32) * (q.shape[-1] ** -0.5)
    return z + bias.astype(jnp.float32)[None, :, None, None]


def sb_log_weights(z, mask):
    log_beta = jax.nn.log_sigmoid(z)
    log_keep = jnp.where(mask, jax.nn.log_sigmoid(-z), 0.0)
    suffix = lax.cumsum(log_keep, axis=z.ndim - 1, reverse=True) - log_keep
    return log_beta + suffix, log_keep


def sb_attention_prompt(q, k, v, bias):
    b, s, h, dh = q.shape
    n_blk = s // Q_BLOCK
    q_blocks = q.reshape(b, n_blk, Q_BLOCK, h, dh).transpose(1, 0, 2, 3, 4)
    key_pos = jnp.arange(s)
    vf = v.astype(jnp.float32)

    def one_block(args):
        qb, blk = args
        z = sb_logits(qb, k, bias)
        q_pos = blk * Q_BLOCK + jnp.arange(Q_BLOCK)
        mask = key_pos[None, :] < q_pos[:, None]
        logw, _ = sb_log_weights(z, mask)
        w = jnp.where(mask, jnp.exp(logw), 0.0)
        return jnp.einsum('bhqk,bkhd->bqhd', w, vf)

    o = lax.map(one_block, (q_blocks, jnp.arange(n_blk)))
    return o.transpose(1, 0, 2, 3, 4).reshape(b, s, h, dh).astype(q.dtype)


def sb_attention_sample(q, k_new, v_new, bias, cache_k, cache_v, page_table, layer):
    b, t, h, dh = q.shape
    z_new = sb_logits(q, k_new, bias)
    pos = jnp.arange(t)
    mask_new = pos[None, :] < pos[:, None]
    logw_new, log_keep_new = sb_log_weights(z_new, mask_new)
    w_new = jnp.where(mask_new, jnp.exp(logw_new), 0.0)
    acc0 = jnp.einsum('bhqk,bkhd->bqhd', w_new, v_new.astype(jnp.float32))
    suffix0 = jnp.sum(log_keep_new, axis=-1)

    def page_step(carry, phys):
        acc, suffix = carry
        kp = cache_k[layer, phys]
        vp = cache_v[layer, phys]
        z = sb_logits(q, kp, bias)
        log_keep = jax.nn.log_sigmoid(-z)
        within = lax.cumsum(log_keep, axis=3, reverse=True) - log_keep
        w = jnp.exp(jax.nn.log_sigmoid(z) + within + suffix[..., None])
        acc = acc + jnp.einsum('bhqk,bkhd->bqhd', w, vp.astype(jnp.float32))
        return (acc, suffix + jnp.sum(log_keep, axis=-1)), None

    (acc, _), _ = lax.scan(page_step, (acc0, suffix0), page_table.T, reverse=True)
    return acc.astype(q.dtype)


def decoder_stack(x, c, conv_hist, ffn_hist, attend, w_ada, b_ada, norm_mix_g, w_in, sb_bias, conv_w, conv_b,
                  w_br_conv, w_br_attn, w_o, norm_ffn_g, w_up, ffn_conv_w, ffn_conv_b, w_down, norm_final_g):
    bsz, t, _ = x.shape
    sizes = [D_CONV] * 3 + [D_ATTN] * 3 + [D_MODEL] * 2
    points = []
    off = 0
    for sz in sizes[:-1]:
        off += sz
        points.append(off)
    c_act = jax.nn.silu(c)
    k_rows, v_rows, conv_states, ffn_states = [], [], [], []
    for l in range(DEPTH):
        mod = c_act @ w_ada[l] + b_ada[l]
        sh_m, sc_m, g_m, sh_f, sc_f, g_f = jnp.split(mod, N_MOD, axis=-1)
        hm = modulate(rms_norm(x, norm_mix_g[l]), sh_m, sc_m)
        proj = hm @ w_in[l]
        gate_b, gate_c, u_in, q, k, v, sel_conv, sel_attn = jnp.split(proj, points, axis=-1)
        u = gate_c * u_in
        u_ext = jnp.concatenate([conv_hist[l], u], axis=1)
        y_conv = gate_b * causal_dwconv(u_ext, conv_w[l], conv_b[l])
        conv_states.append(u_ext[:, -(CONV_W - 1):])
        q = q.reshape(bsz, t, N_HEADS, HEAD_DIM)
        k = k.reshape(bsz, t, N_HEADS, HEAD_DIM)
        v = v.reshape(bsz, t, N_HEADS, HEAD_DIM)
        o_attn = attend(l, q, k, v, sb_bias[l]).reshape(bsz, t, D_ATTN)
        k_rows.append(k)
        v_rows.append(v)
        merged = (jax.nn.sigmoid(sel_conv) * (y_conv @ w_br_conv[l])
                  + jax.nn.sigmoid(sel_attn) * (o_attn @ w_br_attn[l]))
        x = x + g_m[:, None, :] * (merged @ w_o[l])
        hf = modulate(rms_norm(x, norm_ffn_g[l]), sh_f, sc_f)
        up = hf @ w_up[l]
        up_ext = jnp.concatenate([ffn_hist[l], up], axis=1)
        up_c = causal_dwconv(up_ext, ffn_conv_w[l], ffn_conv_b[l])
        ffn_states.append(up_ext[:, -(FFN_CONV_W - 1):])
        a, g = jnp.split(up_c, 2, axis=-1)
        x = x + g_f[:, None, :] * ((jax.nn.silu(g) * a) @ w_down[l])
    y = rms_norm(x, norm_final_g)
    return y, jnp.stack(k_rows), jnp.stack(v_rows), jnp.stack(conv_states), jnp.stack(ffn_states)


def setup_inputs(seed: int = 0) -> dict:
    key = jax.random.key(seed)
    ks = jax.random.split(key, 32)
    f32 = jnp.float32
    n_pages = PAST_LEN // PAGE_SIZE
    n_used = DEC_BATCH * n_pages
    n_pool = n_used + max(1, n_used // 4)

    def nrm(k, shape, scale):
        return jax.random.normal(k, shape, f32) * scale

    x_prompt = nrm(ks[0], (BATCH, SEQ, D_MODEL), 1.0)
    x_sample = nrm(ks[1], (DEC_BATCH, DEC_SEQ, D_MODEL), 1.0)
    cache_k = nrm(ks[2], (DEPTH, n_pool, PAGE_SIZE, N_HEADS, HEAD_DIM), 1.0)
    cache_v = nrm(ks[3], (DEPTH, n_pool, PAGE_SIZE, N_HEADS, HEAD_DIM), 1.0)
    state_conv = nrm(ks[4], (DEPTH, DEC_BATCH, CONV_W - 1, D_CONV), 1.0)
    state_ffn = nrm(ks[5], (DEPTH, DEC_BATCH, FFN_CONV_W - 1, 2 * D_FF), 1.0)
    page_table = jax.random.permutation(ks[6], n_pool)[:n_used].reshape(DEC_BATCH, n_pages).astype(jnp.int32)
    c_prompt = nrm(ks[7], (BATCH, D_MODEL), 1.0)
    c_sample = nrm(ks[8], (DEC_BATCH, D_MODEL), 1.0)
    w_ada = nrm(ks[9], (DEPTH, D_MODEL, N_MOD * D_MODEL), 0.5 * D_MODEL ** -0.5)
    b_ada = nrm(ks[10], (DEPTH, N_MOD * D_MODEL), 0.02)
    norm_mix_g = 1.0 + nrm(ks[11], (DEPTH, D_MODEL), 0.02)
    w_in = nrm(ks[12], (DEPTH, D_MODEL, D_IN), D_MODEL ** -0.5)
    sb_bias = (-jnp.linspace(SB_BIAS_LO, SB_BIAS_HI, N_HEADS, dtype=f32)[None, :]
               + nrm(ks[24], (DEPTH, N_HEADS), 0.1))
    conv_w = nrm(ks[13], (DEPTH, CONV_W, D_CONV), CONV_W ** -0.5)
    conv_b = nrm(ks[14], (DEPTH, D_CONV), 0.02)
    w_br_conv = nrm(ks[15], (DEPTH, D_CONV, D_MODEL), D_CONV ** -0.5)
    w_br_attn = nrm(ks[16], (DEPTH, D_ATTN, D_MODEL), D_ATTN ** -0.5)
    w_o = nrm(ks[17], (DEPTH, D_MODEL, D_MODEL), D_MODEL ** -0.5)
    norm_ffn_g = 1.0 + nrm(ks[18], (DEPTH, D_MODEL), 0.02)
    w_up = nrm(ks[19], (DEPTH, D_MODEL, 2 * D_FF), D_MODEL ** -0.5)
    ffn_conv_w = nrm(ks[20], (DEPTH, FFN_CONV_W, 2 * D_FF), FFN_CONV_W ** -0.5)
    ffn_conv_b = nrm(ks[21], (DEPTH, 2 * D_FF), 0.02)
    w_down = nrm(ks[22], (DEPTH, D_FF, D_MODEL), D_FF ** -0.5)
    norm_final_g = 1.0 + nrm(ks[23], (D_MODEL,), 0.02)
    return {'x_prompt': x_prompt, 'x_sample': x_sample, 'cache_k': cache_k, 'cache_v': cache_v,
            'state_conv': state_conv, 'state_ffn': state_ffn, 'page_table': page_table,
            'c_prompt': c_prompt, 'c_sample': c_sample, 'w_ada': w_ada, 'b_ada': b_ada,
            'norm_mix_g': norm_mix_g, 'w_in': w_in, 'sb_bias': sb_bias, 'conv_w': conv_w, 'conv_b': conv_b,
            'w_br_conv': w_br_conv, 'w_br_attn': w_br_attn, 'w_o': w_o, 'norm_ffn_g': norm_ffn_g,
            'w_up': w_up, 'ffn_conv_w': ffn_conv_w, 'ffn_conv_b': ffn_conv_b, 'w_down': w_down,
            'norm_final_g': norm_final_g}


def reference(x_prompt, x_sample, cache_k, cache_v, state_conv, state_ffn, page_table, c_prompt, c_sample,
              w_ada, b_ada, norm_mix_g, w_in, sb_bias, conv_w, conv_b, w_br_conv, w_br_attn, w_o, norm_ffn_g,
              w_up, ffn_conv_w, ffn_conv_b, w_down, norm_final_g):
    weights = (w_ada, b_ada, norm_mix_g, w_in, sb_bias, conv_w, conv_b, w_br_conv, w_br_attn, w_o, norm_ffn_g,
               w_up, ffn_conv_w, ffn_conv_b, w_down, norm_final_g)
    bp = x_prompt.shape[0]
    zero_conv = jnp.zeros((DEPTH, bp, CONV_W - 1, D_CONV), x_prompt.dtype)
    zero_ffn = jnp.zeros((DEPTH, bp, FFN_CONV_W - 1, 2 * D_FF), x_prompt.dtype)

    def attend_prompt(l, q, k, v, bias):
        return sb_attention_prompt(q, k, v, bias)

    def attend_sample(l, q, k, v, bias):
        return sb_attention_sample(q, k, v, bias, cache_k, cache_v, page_table, l)

    y_prompt, k_prompt, v_prompt, conv_state_prompt, ffn_state_prompt = decoder_stack(
        x_prompt, c_prompt, zero_conv, zero_ffn, attend_prompt, *weights)
    y_sample, k_sample, v_sample, conv_state_sample, ffn_state_sample = decoder_stack(
        x_sample, c_sample, state_conv, state_ffn, attend_sample, *weights)
    return (y_prompt, y_sample, k_prompt, v_prompt, conv_state_prompt, ffn_state_prompt,
            k_sample, v_sample, conv_state_sample, ffn_state_sample)
```

```python
import functools

import jax
import jax.numpy as jnp
from jax import lax
from jax.experimental import pallas as pl
from jax.experimental.pallas import tpu as pltpu

F32 = jnp.float32
BF16 = jnp.bfloat16

EPS = 1e-6
N_MOD = 6
HEAD_DIM = 64
HEADS_PER_LANE_TILE = 2
V7X_LANES = 128
V7X_SUBLANES = 8
V7X_VMEM_LIMIT = 56 * 1024 * 1024

ROW_TILE = 512
FFN_CHUNK = 256
Q_TILE = 256
K_TILE = 128
PAGES_PER_STEP = 4


def _resident(block_shape, index_map):
    return pl.BlockSpec(block_shape, index_map, pipeline_mode=pl.Buffered(1))


def _expand_rows(m, rows):
    r = m.shape[0]
    if r == 1 or r == rows:
        return m
    return jnp.concatenate([m] * (rows // r), axis=0)


def _rms_norm(x, g):
    ms = jnp.mean(x * x, axis=-1, keepdims=True)
    return x * lax.rsqrt(ms + EPS) * g


def _softplus(z):
    return jnp.maximum(z, 0.0) + jnp.log(1.0 + jnp.exp(-jnp.abs(z)))


def _split_bf16(x):
    hi = x.astype(BF16)
    lo = (x - hi.astype(F32)).astype(BF16)
    return hi, lo


def _dot(a, b):
    return jnp.dot(a, b, preferred_element_type=F32)


def _dot_nt(a, b):
    return lax.dot_general(a, b, (((1,), (1,)), ((), ())), preferred_element_type=F32)


def _suffix_matrix(k_tile):
    j = jnp.arange(k_tile)[:, None]
    s = jnp.arange(k_tile)[None, :]
    u = (j >= s).astype(BF16)
    half = jnp.concatenate([u, jnp.ones((k_tile, k_tile), BF16)], axis=1)
    return jnp.concatenate([half, half], axis=0)


def _causal_dwconv(ext_ref, pad, slab, rows, w_ref, b_ref, col0, width):
    cols = pl.ds(col0, width)
    y = b_ref[:, cols] + w_ref[2:3, cols] * ext_ref[pad:pad + rows, :]
    y = y + w_ref[1:2, cols] * ext_ref[pad - slab:pad - slab + rows, :]
    y = y + w_ref[0:1, cols] * ext_ref[pad - 2 * slab:pad - 2 * slab + rows, :]
    return y


def _ada_kernel(c_ref, w_ref, b_ref, o_ref):
    c = c_ref[...]
    ca = (c * jax.nn.sigmoid(c)).astype(BF16)
    o_ref[...] = _dot(ca, w_ref[...]) + b_ref[...]


def _ada_call(c_all, w_ada, b_ada):
    n, d = c_all.shape
    dn = w_ada.shape[1]
    tn = d
    return pl.pallas_call(
        _ada_kernel,
        grid=(dn // tn,),
        in_specs=[pl.BlockSpec((n, d), lambda j: (0, 0)),
                  pl.BlockSpec((d, tn), lambda j: (0, j)),
                  pl.BlockSpec((1, tn), lambda j: (0, j))],
        out_specs=pl.BlockSpec((n, tn), lambda j: (0, j)),
        out_shape=jax.ShapeDtypeStruct((n, dn), F32),
        compiler_params=pltpu.CompilerParams(dimension_semantics=("arbitrary",),
                                             vmem_limit_bytes=V7X_VMEM_LIMIT),
        name="ada_mod",
    )(c_all, w_ada, b_ada)


def _mix_in_kernel(x_ref, mod_ref, hist_ref, g_ref, w_in_ref, wkvt_ref, cw_ref, cb_ref, wbc_ref,
                   ktf_ref, vtf_ref, qb_ref, kb_ref, vb_ref, mconv_ref, sga_ref, cst_ref,
                   ext_ref, *, slab, pad, d_conv, d_attn, d_model, rowmajor_kv):
    t = pl.program_id(1)
    rows = x_ref.shape[0]

    @pl.when(t == 0)
    def _():
        ext_ref[0:pad, :] = jnp.zeros((pad, d_conv), F32)
        ext_ref[pad - 2 * slab:pad, :] = hist_ref[...]

    shift = _expand_rows(mod_ref[0], rows)
    scale = _expand_rows(mod_ref[1], rows)
    hm = _rms_norm(x_ref[...], g_ref[...]) * (1.0 + scale) + shift
    hb = hm.astype(BF16)

    def proj(c0, width):
        return _dot(hb, w_in_ref[:, c0:c0 + width])

    gate_b = proj(0, d_conv)
    gate_c = proj(d_conv, d_conv)
    u_in = proj(2 * d_conv, d_conv)
    ext_ref[pad:pad + rows, :] = gate_c * u_in
    y_conv = gate_b * _causal_dwconv(ext_ref, pad, slab, rows, cw_ref, cb_ref, 0, d_conv)
    cst_ref[...] = ext_ref[pad + rows - 2 * slab:pad + rows, :]
    ext_ref[0:pad, :] = ext_ref[rows:rows + pad, :]

    off = 3 * d_conv
    q = proj(off, d_attn)
    qb_ref[...] = (q * (HEAD_DIM ** -0.5)).astype(BF16)
    kvt = _dot_nt(wkvt_ref[...], hb)
    ktf_ref[...] = kvt[0:d_attn]
    vtf_ref[...] = kvt[d_attn:2 * d_attn]
    if rowmajor_kv:
        kb_ref[...] = proj(off + d_attn, d_attn).astype(BF16)
        vb_ref[...] = proj(off + 2 * d_attn, d_attn).astype(BF16)
    else:
        for j in range(rows // K_TILE):
            kb_ref[j] = kvt[0:d_attn, j * K_TILE:(j + 1) * K_TILE].astype(BF16)
            vb_ref[j] = kvt[d_attn:2 * d_attn, j * K_TILE:(j + 1) * K_TILE].astype(BF16)

    off = off + 3 * d_attn
    sel_conv = proj(off, d_model)
    br_conv = _dot(y_conv.astype(BF16), wbc_ref[...])
    mconv_ref[...] = (jax.nn.sigmoid(sel_conv) * br_conv).astype(BF16)
    sel_attn = proj(off + d_model, d_model)
    sga_ref[...] = jax.nn.sigmoid(sel_attn).astype(BF16)


def _mix_in_call(x, mod, hist, g, w_in, w_kvt, conv_w, conv_b, w_br_conv, *, slab, d_attn,
                 rowmajor_kv):
    nb, rows_total, d_model = x.shape
    d_conv = conv_w.shape[1]
    d_in = w_in.shape[1]
    tm = min(ROW_TILE, rows_total)
    nt = rows_total // tm
    r = mod.shape[2]
    pad = max(V7X_SUBLANES, 2 * slab)

    def row_spec(width):
        return pl.BlockSpec((None, tm, width), lambda b, t: (b, t, 0))

    def out(width, dtype):
        return jax.ShapeDtypeStruct((nb, rows_total, width), dtype)

    col_spec = pl.BlockSpec((None, d_attn, tm), lambda b, t: (b, 0, t))
    col_out = jax.ShapeDtypeStruct((nb, d_attn, rows_total), F32)
    if rowmajor_kv:
        kv_spec, kv_out = row_spec(d_attn), out(d_attn, BF16)
    else:
        kv_spec = pl.BlockSpec((None, tm // K_TILE, d_attn, K_TILE), lambda b, t: (b, t, 0, 0))
        kv_out = jax.ShapeDtypeStruct((nb, rows_total // K_TILE, d_attn, K_TILE), BF16)

    kern = functools.partial(_mix_in_kernel, slab=slab, pad=pad, d_conv=d_conv,
                             d_attn=d_attn, d_model=d_model, rowmajor_kv=rowmajor_kv)
    return pl.pallas_call(
        kern,
        grid=(nb, nt),
        in_specs=[row_spec(d_model),
                  pl.BlockSpec((None, N_MOD, r, d_model), lambda b, t: (b, 0, 0, 0)),
                  pl.BlockSpec((None, 2 * slab, d_conv), lambda b, t: (b, 0, 0)),
                  _resident((1, d_model), lambda b, t: (0, 0)),
                  _resident((d_model, d_in), lambda b, t: (0, 0)),
                  _resident((2 * d_attn, d_model), lambda b, t: (0, 0)),
                  _resident((3, d_conv), lambda b, t: (0, 0)),
                  _resident((1, d_conv), lambda b, t: (0, 0)),
                  _resident((d_conv, d_model), lambda b, t: (0, 0))],
        out_specs=[col_spec, col_spec, row_spec(d_attn), kv_spec, kv_spec,
                   row_spec(d_model), row_spec(d_model),
                   pl.BlockSpec((None, 2 * slab, d_conv), lambda b, t: (b, 0, 0))],
        out_shape=[col_out, col_out, out(d_attn, BF16), kv_out, kv_out,
                   out(d_model, BF16), out(d_model, BF16),
                   jax.ShapeDtypeStruct((nb, 2 * slab, d_conv), F32)],
        scratch_shapes=[pltpu.VMEM((pad + tm, d_conv), F32)],
        compiler_params=pltpu.CompilerParams(dimension_semantics=("arbitrary", "arbitrary"),
                                             vmem_limit_bytes=V7X_VMEM_LIMIT),
        name="mix_in",
    )(x, mod, hist, g, w_in, w_kvt, conv_w, conv_b, w_br_conv)


def _mix_out_kernel(x_ref, o_ref, mconv_ref, sga_ref, mod_ref, hist_ref, wba_ref, wo_ref,
                    gf_ref, wup_ref, fw_ref, fb_ref, wdn_ref, gfin_ref,
                    y_ref, fst_ref,
                    car_ref, exta_ref, extg_ref, acc_ref, hf_ref,
                    *, slab, pad, d_ff):
    t = pl.program_id(1)
    rows = x_ref.shape[0]

    @pl.when(t == 0)
    def _():
        car_ref[...] = jnp.zeros(car_ref.shape, F32)
        car_ref[pad - 2 * slab:pad, :] = hist_ref[...]

    br_attn = _dot(o_ref[...], wba_ref[...])
    merged = mconv_ref[...].astype(F32) + sga_ref[...].astype(F32) * br_attn
    gate_m = _expand_rows(mod_ref[2], rows)
    x1 = x_ref[...] + gate_m * _dot(merged.astype(BF16), wo_ref[...])

    shift = _expand_rows(mod_ref[3], rows)
    scale = _expand_rows(mod_ref[4], rows)
    hf_ref[...] = (_rms_norm(x1, gf_ref[...]) * (1.0 + scale) + shift).astype(BF16)

    for c in range(d_ff // FFN_CHUNK):
        ca = c * FFN_CHUNK
        cg = d_ff + ca
        hf = hf_ref[...]
        exta_ref[0:pad, :] = car_ref[:, ca:ca + FFN_CHUNK]
        extg_ref[0:pad, :] = car_ref[:, cg:cg + FFN_CHUNK]
        exta_ref[pad:pad + rows, :] = _dot(hf, wup_ref[:, ca:ca + FFN_CHUNK])
        extg_ref[pad:pad + rows, :] = _dot(hf, wup_ref[:, cg:cg + FFN_CHUNK])
        a = _causal_dwconv(exta_ref, pad, slab, rows, fw_ref, fb_ref, ca, FFN_CHUNK)
        g = _causal_dwconv(extg_ref, pad, slab, rows, fw_ref, fb_ref, cg, FFN_CHUNK)
        car_ref[:, ca:ca + FFN_CHUNK] = exta_ref[rows:rows + pad, :]
        car_ref[:, cg:cg + FFN_CHUNK] = extg_ref[rows:rows + pad, :]
        h = (g * jax.nn.sigmoid(g) * a).astype(BF16)
        part = _dot(h, wdn_ref[ca:ca + FFN_CHUNK, :])
        if c == 0:
            acc_ref[...] = part
        else:
            acc_ref[...] += part

    fst_ref[...] = car_ref[pad - 2 * slab:pad, :]
    gate_f = _expand_rows(mod_ref[5], rows)
    x2 = x1 + gate_f * acc_ref[...]
    y_ref[...] = _rms_norm(x2, gfin_ref[...])


def _mix_out_call(x, o_attn, mconv, sga, mod, hist, w_br_attn, w_o, g_ffn, w_up, ffn_w, ffn_b,
                  w_down, g_final, *, slab):
    nb, rows_total, d_model = x.shape
    d_attn = o_attn.shape[2]
    d_ff = w_down.shape[0]
    tm = min(ROW_TILE, rows_total)
    nt = rows_total // tm
    r = mod.shape[2]
    pad = max(V7X_SUBLANES, 2 * slab)

    def row_spec(width):
        return pl.BlockSpec((None, tm, width), lambda b, t: (b, t, 0))

    def const(shape):
        return _resident(shape, lambda b, t: (0,) * len(shape))

    kern = functools.partial(_mix_out_kernel, slab=slab, pad=pad, d_ff=d_ff)
    return pl.pallas_call(
        kern,
        grid=(nb, nt),
        in_specs=[row_spec(d_model), row_spec(d_attn), row_spec(d_model), row_spec(d_model),
                  pl.BlockSpec((None, N_MOD, r, d_model), lambda b, t: (b, 0, 0, 0)),
                  pl.BlockSpec((None, 2 * slab, 2 * d_ff), lambda b, t: (b, 0, 0)),
                  const((d_attn, d_model)), const((d_model, d_model)), const((1, d_model)),
                  const((d_model, 2 * d_ff)), const((3, 2 * d_ff)), const((1, 2 * d_ff)),
                  const((d_ff, d_model)), const((1, d_model))],
        out_specs=[row_spec(d_model),
                   pl.BlockSpec((None, 2 * slab, 2 * d_ff), lambda b, t: (b, 0, 0))],
        out_shape=[jax.ShapeDtypeStruct((nb, rows_total, d_model), F32),
                   jax.ShapeDtypeStruct((nb, 2 * slab, 2 * d_ff), F32)],
        scratch_shapes=[pltpu.VMEM((pad, 2 * d_ff), F32),
                        pltpu.VMEM((pad + tm, FFN_CHUNK), F32),
                        pltpu.VMEM((pad + tm, FFN_CHUNK), F32),
                        pltpu.VMEM((tm, d_model), F32),
                        pltpu.VMEM((tm, d_model), BF16)],
        compiler_params=pltpu.CompilerParams(dimension_semantics=("arbitrary", "arbitrary"),
                                             vmem_limit_bytes=V7X_VMEM_LIMIT),
        name="mix_out_ffn",
    )(x, o_attn, mconv, sga, mod, hist, w_br_attn, w_o, g_ffn, w_up, ffn_w, ffn_b, w_down,
      g_final)


def _sb_block(qh, bias, kblk, vblk, u2, carry, mask, keys_on_lanes):
    z = (_dot(qh, kblk) if keys_on_lanes else _dot_nt(qh, kblk)) + bias
    nlk = _softplus(z)
    if mask is not None:
        nlk = jnp.where(mask, nlk, 0.0)
    hi, lo = _split_bf16(nlk)
    s2 = _dot(jnp.concatenate([hi, lo], axis=1), u2)
    w = jnp.exp(z - s2[:, :K_TILE] - carry)
    if mask is not None:
        w = jnp.where(mask, w, 0.0)
    w = w.astype(BF16)
    pv = _dot_nt(w, vblk) if keys_on_lanes else _dot(w, vblk)
    return pv, carry + s2[:, K_TILE:]


def _attn_prompt_kernel(bias_ref, q_ref, k_ref, v_ref, u2_ref, o_ref, acc_ref, car_ref):
    hp = pl.program_id(1)
    i = pl.program_id(2)
    tq = q_ref.shape[0]
    nd = tq // K_TILE
    q = q_ref[...]
    lane = lax.broadcasted_iota(jnp.int32, (1, V7X_LANES), 1)
    zero = jnp.zeros_like(q)
    qh = [jnp.where(lane < HEAD_DIM, q, zero), jnp.where(lane >= HEAD_DIM, q, zero)]
    bias = [bias_ref[HEADS_PER_LANE_TILE * hp], bias_ref[HEADS_PER_LANE_TILE * hp + 1]]
    u2 = u2_ref[...]
    acc_ref[...] = jnp.zeros(acc_ref.shape, F32)
    car_ref[...] = jnp.zeros(car_ref.shape, F32)

    def sweep(kb, mask):
        kblk = k_ref[kb]
        vblk = v_ref[kb]
        for h in range(HEADS_PER_LANE_TILE):
            pv, car = _sb_block(qh[h], bias[h], kblk, vblk, u2, car_ref[h], mask, True)
            acc_ref[h] += pv
            car_ref[h] = car

    row = lax.broadcasted_iota(jnp.int32, (tq, K_TILE), 0)
    col = lax.broadcasted_iota(jnp.int32, (tq, K_TILE), 1)
    for d in range(nd - 1, -1, -1):
        sweep(i * nd + d, col + d * K_TILE < row)

    def past(n, _):
        sweep(i * nd - (n + 1), None)
        return 0

    lax.fori_loop(0, i * nd, past, 0)
    o_ref[...] = jnp.where(lane < HEAD_DIM, acc_ref[0], acc_ref[1]).astype(o_ref.dtype)


def _attn_prompt_call(qb, kb, vb, sb_bias, u2):
    b, s, d_attn = qb.shape
    tq = min(Q_TILE, s)
    grid = (b, d_attn // V7X_LANES, s // tq)
    kv_spec = pl.BlockSpec((None, s // K_TILE, V7X_LANES, K_TILE), lambda bb, hp, i: (bb, 0, hp, 0))
    return pl.pallas_call(
        _attn_prompt_kernel,
        grid=grid,
        in_specs=[pl.BlockSpec(memory_space=pltpu.SMEM),
                  pl.BlockSpec((None, tq, V7X_LANES), lambda bb, hp, i: (bb, i, hp)),
                  kv_spec, kv_spec,
                  _resident(u2.shape, lambda bb, hp, i: (0, 0))],
        out_specs=pl.BlockSpec((None, tq, V7X_LANES), lambda bb, hp, i: (bb, i, hp)),
        out_shape=jax.ShapeDtypeStruct((b, s, d_attn), BF16),
        scratch_shapes=[pltpu.VMEM((HEADS_PER_LANE_TILE, tq, V7X_LANES), F32),
                        pltpu.VMEM((HEADS_PER_LANE_TILE, tq, K_TILE), F32)],
        compiler_params=pltpu.CompilerParams(
            dimension_semantics=("arbitrary", "arbitrary", "arbitrary"),
            vmem_limit_bytes=V7X_VMEM_LIMIT),
        name="sb_attn_prompt",
    )(sb_bias, qb, kb, vb, u2)


def _attn_sample_kernel(pt_ref, q_ref, kn_ref, vn_ref, bias_ref, hmask_ref, nmask_ref, sel_ref,
                        u2_ref, *rest, n_pages_step):
    page_refs = rest[:2 * n_pages_step]
    o_ref, acc_ref, car_ref = rest[2 * n_pages_step:]
    s = pl.program_id(1)
    hmask = hmask_ref[...]
    qbd = q_ref[...] * hmask.astype(BF16)
    bias = bias_ref[...]
    u2 = u2_ref[...]

    def sweep(kblk, vblk, mask, keys_on_lanes):
        pv, car = _sb_block(qbd, bias, kblk, vblk, u2, car_ref[...], mask, keys_on_lanes)
        acc_ref[...] += pv
        car_ref[...] = car

    @pl.when(s == 0)
    def _():
        acc_ref[...] = jnp.zeros(acc_ref.shape, F32)
        car_ref[...] = jnp.zeros(car_ref.shape, F32)
        fill = jnp.zeros((K_TILE - kn_ref.shape[0], kn_ref.shape[1]), BF16)
        sweep(jnp.concatenate([kn_ref[...], fill], axis=0),
              jnp.concatenate([vn_ref[...], fill], axis=0), nmask_ref[...] > 0.0, False)

    for p in range(n_pages_step):
        sweep(page_refs[p][...].astype(BF16), page_refs[n_pages_step + p][...].astype(BF16),
              None, True)

    @pl.when(s == pl.num_programs(1) - 1)
    def _():
        hi, lo = _split_bf16(acc_ref[...] * hmask)
        sel = sel_ref[...]
        o_ref[...] = (_dot(sel, hi) + _dot(sel, lo))[0:o_ref.shape[0], :]


def _attn_sample_call(q_exp, k_new, v_new, cache_k, cache_v, page_table, bias_rows, hmask, nmask,
                      sel, u2, t_new):
    bd, rows, d_attn = q_exp.shape
    n_pages = page_table.shape[1]
    page = cache_k.shape[2]
    pps = PAGES_PER_STEP if n_pages % PAGES_PER_STEP == 0 else 1
    steps = n_pages // pps

    def const(a):
        return _resident(a.shape, lambda b, s, pt: (0,) * a.ndim)

    def page_spec(p):
        def index(b, s, pt):
            return (pt[b * n_pages + (n_pages - 1 - (s * pps + p))], 0, 0)
        return pl.BlockSpec((None, d_attn, page), index)

    new_spec = pl.BlockSpec((None, k_new.shape[1], d_attn), lambda b, s, pt: (b, 0, 0))
    kern = functools.partial(_attn_sample_kernel, n_pages_step=pps)
    return pl.pallas_call(
        kern,
        grid_spec=pltpu.PrefetchScalarGridSpec(
            num_scalar_prefetch=1,
            grid=(bd, steps),
            in_specs=[pl.BlockSpec((None, rows, d_attn), lambda b, s, pt: (b, 0, 0)),
                      new_spec, new_spec, const(bias_rows), const(hmask), const(nmask),
                      const(sel), const(u2)]
                     + [page_spec(p) for p in range(pps)] * 2,
            out_specs=pl.BlockSpec((None, t_new, d_attn), lambda b, s, pt: (b, 0, 0)),
            scratch_shapes=[pltpu.VMEM((rows, d_attn), F32), pltpu.VMEM((rows, K_TILE), F32)]),
        out_shape=jax.ShapeDtypeStruct((bd, t_new, d_attn), F32),
        compiler_params=pltpu.CompilerParams(dimension_semantics=("arbitrary", "arbitrary"),
                                             vmem_limit_bytes=V7X_VMEM_LIMIT),
        name="sb_attn_sample",
    )(page_table.reshape(-1), q_exp, k_new, v_new, bias_rows, hmask, nmask, sel, u2,
      *([cache_k] * pps), *([cache_v] * pps))


def kernel(x_prompt, x_sample, cache_k, cache_v, state_conv, state_ffn, page_table, c_prompt, c_sample, w_ada, b_ada, norm_mix_g, w_in, sb_bias, conv_w, conv_b, w_br_conv, w_br_attn, w_o, norm_ffn_g, w_up, ffn_conv_w, ffn_conv_b, w_down, norm_final_g):
    assert w_ada.shape[0] == 1, "single-layer step"
    b, s, d_model = x_prompt.shape
    bd, t_new, _ = x_sample.shape
    n_heads = sb_bias.shape[1]
    d_attn = n_heads * HEAD_DIM
    d_conv = conv_w.shape[2]
    d_ff2 = w_up.shape[2]
    page = cache_k.shape[2]
    assert page == K_TILE and d_attn % V7X_LANES == 0

    w_ada_b, w_in_b, w_brc_b, w_bra_b, w_o_b, w_up_b, w_dn_b = (
        w[0].astype(BF16) for w in (w_ada, w_in, w_br_conv, w_br_attn, w_o, w_up, w_down))
    g_mix, g_ffn, g_fin = norm_mix_g[0][None], norm_ffn_g[0][None], norm_final_g[None]
    cw, cb = conv_w[0], conv_b[0][None]
    fw, fb = ffn_conv_w[0], ffn_conv_b[0][None]
    bias = sb_bias[0]
    u2 = _suffix_matrix(K_TILE)

    mod = _ada_call(jnp.concatenate([c_prompt, c_sample], axis=0), w_ada_b, b_ada[0][None])
    mod_p = mod[:b].reshape(b, N_MOD, 1, d_model)
    mod_s = mod[b:].reshape(1, bd, N_MOD, d_model).transpose(0, 2, 1, 3)

    kv0 = 3 * d_conv + d_attn
    w_kvt = w_in_b[:, kv0:kv0 + 2 * d_attn].T

    ktf, vtf, qb, kb, vb, mconv, sga, cst_p = _mix_in_call(
        x_prompt, mod_p, jnp.zeros((b, 2, d_conv), F32), g_mix, w_in_b, w_kvt, cw, cb, w_brc_b,
        slab=1, d_attn=d_attn, rowmajor_kv=False)
    o_p = _attn_prompt_call(qb, kb, vb, bias, u2)
    y_p, fst_p = _mix_out_call(x_prompt, o_p, mconv, sga, mod_p, jnp.zeros((b, 2, d_ff2), F32),
                               w_bra_b, w_o_b, g_ffn, w_up_b, fw, fb, w_dn_b, g_fin, slab=1)

    def time_major(a):
        return a.transpose(1, 0, 2).reshape(1, a.shape[1] * bd, a.shape[2])

    def batch_major(a, steps):
        return a.reshape(steps, bd, a.shape[-1]).transpose(1, 0, 2)

    xs = time_major(x_sample)
    ktf_s, vtf_s, qb_s, kb_s, vb_s, mconv_s, sga_s, cst_s = _mix_in_call(
        xs, mod_s, time_major(state_conv[0]), g_mix, w_in_b, w_kvt, cw, cb, w_brc_b,
        slab=bd, d_attn=d_attn, rowmajor_kv=True)

    rows = n_heads * t_new
    q_exp = jnp.tile(batch_major(qb_s, t_new)[:, None], (1, n_heads, 1, 1)).reshape(bd, rows, d_attn)
    new_pad = 16 - t_new
    k_new = jnp.pad(batch_major(kb_s, t_new), ((0, 0), (0, new_pad), (0, 0)))
    v_new = jnp.pad(batch_major(vb_s, t_new), ((0, 0), (0, new_pad), (0, 0)))
    r_head = jnp.arange(rows) // t_new
    r_time = jnp.arange(rows) % t_new
    bias_rows = jnp.broadcast_to(bias[r_head][:, None], (rows, K_TILE)).astype(F32)
    hmask = (jnp.arange(d_attn)[None, :] // HEAD_DIM == r_head[:, None]).astype(F32)
    nmask = (jnp.arange(K_TILE)[None, :] < r_time[:, None]).astype(F32)
    sel = (jnp.arange(V7X_SUBLANES)[:, None] == r_time[None, :]).astype(BF16)
    def page_layout(cache):
        return cache[0].transpose(0, 2, 3, 1).reshape(-1, d_attn, page)

    o_s = _attn_sample_call(q_exp, k_new, v_new, page_layout(cache_k), page_layout(cache_v),
                            page_table, bias_rows, hmask, nmask, sel, u2, t_new)
    y_s, fst_s = _mix_out_call(xs, time_major(o_s.astype(BF16)), mconv_s, sga_s, mod_s,
                               time_major(state_ffn[0]), w_bra_b, w_o_b, g_ffn, w_up_b, fw, fb,
                               w_dn_b, g_fin, slab=bd)

    k_p =ktf.reshape(b, n_heads, HEAD_DIM, s).transpose(0, 3, 1, 2)[None]
    v_p = vtf.reshape(b, n_heads, HEAD_DIM, s).transpose(0, 3, 1, 2)[None]
    k_s = ktf_s.reshape(n_heads, HEAD_DIM, t_new, bd).transpose(3, 2, 0, 1)[None]
    v_s = vtf_s.reshape(n_heads, HEAD_DIM, t_new, bd).transpose(3, 2, 0, 1)[None]
    return (y_p, batch_major(y_s[0], t_new), k_p, v_p, cst_p[None], fst_p[None], k_s, v_s,
            batch_major(cst_s[0], 2)[None], batch_major(fst_s[0], 2)[None])
```

```python
import functools

import jax
import jax.numpy as jnp
from jax import lax
from jax.experimental import pallas as pl
from jax.experimental.pallas import tpu as pltpu

F32 = jnp.float32
BF16 = jnp.bfloat16

EPS = 1e-6
N_MOD = 6
HEAD_DIM = 64
HEADS_PER_LANE_TILE = 2
V7X_LANES = 128
V7X_SUBLANES = 8
V7X_VMEM_LIMIT = 56 * 1024 * 1024

ROW_TILE = 512
FFN_CHUNK = 256
Q_TILE = 512
UNIT_KEYS = 256
MASKED_LOGIT = -1e30
LOG2E = 1.4426950408889634
K_TILE = 128
PAGES_PER_STEP = 8


def _resident(block_shape, index_map):
    return pl.BlockSpec(block_shape, index_map, pipeline_mode=pl.Buffered(1))


def _expand_rows(m, rows):
    r = m.shape[0]
    if r == 1 or r == rows:
        return m
    return jnp.concatenate([m] * (rows // r), axis=0)


def _rms_norm(x, g):
    ms = jnp.mean(x * x, axis=-1, keepdims=True)
    return x * lax.rsqrt(ms + EPS) * g


def _softplus(z):
    return jnp.maximum(z, 0.0) + jnp.log(1.0 + jnp.exp(-jnp.abs(z)))


def _split_bf16(x):
    hi = x.astype(BF16)
    lo = (x - hi.astype(F32)).astype(BF16)
    return hi, lo


def _dot(a, b):
    return jnp.dot(a, b, preferred_element_type=F32)


def _dot_nt(a, b):
    return lax.dot_general(a, b, (((1,), (1,)), ((), ())), preferred_element_type=F32)


def _inclusive_suffix_matrix(keys):
    j = jnp.arange(keys)[:, None]
    s = jnp.arange(keys)[None, :]
    return (j >= s).astype(BF16)


def _suffix_matrix(k_tile):
    j = jnp.arange(k_tile)[:, None]
    s = jnp.arange(k_tile)[None, :]
    u = (j >= s).astype(BF16)
    half = jnp.concatenate([u, jnp.ones((k_tile, k_tile), BF16)], axis=1)
    return jnp.concatenate([half, half], axis=0)


def _causal_dwconv(ext_ref, pad, slab, rows, w_ref, b_ref, col0, width):
    cols = pl.ds(col0, width)
    y = b_ref[:, cols] + w_ref[2:3, cols] * ext_ref[pad:pad + rows, :]
    y = y + w_ref[1:2, cols] * ext_ref[pad - slab:pad - slab + rows, :]
    y = y + w_ref[0:1, cols] * ext_ref[pad - 2 * slab:pad - 2 * slab + rows, :]
    return y


def _ada_kernel(c_ref, w_ref, b_ref, o_ref):
    c = c_ref[...]
    ca = (c * jax.nn.sigmoid(c)).astype(BF16)
    o_ref[...] = _dot(ca, w_ref[...]) + b_ref[...]


def _ada_call(c_all, w_ada, b_ada):
    n, d = c_all.shape
    dn = w_ada.shape[1]
    tn = d
    return pl.pallas_call(
        _ada_kernel,
        grid=(dn // tn,),
        in_specs=[pl.BlockSpec((n, d), lambda j: (0, 0)),
                  pl.BlockSpec((d, tn), lambda j: (0, j)),
                  pl.BlockSpec((1, tn), lambda j: (0, j))],
        out_specs=pl.BlockSpec((n, tn), lambda j: (0, j)),
        out_shape=jax.ShapeDtypeStruct((n, dn), F32),
        compiler_params=pltpu.CompilerParams(dimension_semantics=("arbitrary",),
                                             vmem_limit_bytes=V7X_VMEM_LIMIT),
        name="ada_mod",
    )(c_all, w_ada, b_ada)


def _mix_in_kernel(x_ref, mod_ref, hist_ref, g_ref, w_in_ref, wkvt_ref, cw_ref, cb_ref, wbc_ref,
                   ktf_ref, vtf_ref, qb_ref, kb_ref, vb_ref, mconv_ref, sga_ref, cst_ref,
                   ext_ref, *, slab, pad, d_conv, d_attn, d_model, rowmajor_kv, q_scale):
    t = pl.program_id(1)
    rows = x_ref.shape[0]

    @pl.when(t == 0)
    def _():
        ext_ref[0:pad, :] = jnp.zeros((pad, d_conv), F32)
        ext_ref[pad - 2 * slab:pad, :] = hist_ref[...]

    shift = _expand_rows(mod_ref[0], rows)
    scale = _expand_rows(mod_ref[1], rows)
    hm = _rms_norm(x_ref[...], g_ref[...]) * (1.0 + scale) + shift
    hb = hm.astype(BF16)

    def proj(c0, width):
        return _dot(hb, w_in_ref[:, c0:c0 + width])

    gate_b = proj(0, d_conv)
    gate_c = proj(d_conv, d_conv)
    u_in = proj(2 * d_conv, d_conv)
    ext_ref[pad:pad + rows, :] = gate_c * u_in
    y_conv = gate_b * _causal_dwconv(ext_ref, pad, slab, rows, cw_ref, cb_ref, 0, d_conv)
    cst_ref[...] = ext_ref[pad + rows - 2 * slab:pad + rows, :]
    ext_ref[0:pad, :] = ext_ref[rows:rows + pad, :]

    off = 3 * d_conv
    q = proj(off, d_attn)
    qb_ref[...] = (q * q_scale).astype(BF16)
    kvt = _dot_nt(wkvt_ref[...], hb)
    ktf_ref[...] = kvt[0:d_attn]
    vtf_ref[...] = kvt[d_attn:2 * d_attn]
    if rowmajor_kv:
        kb_ref[...] = proj(off + d_attn, d_attn).astype(BF16)
        vb_ref[...] = proj(off + 2 * d_attn, d_attn).astype(BF16)
    else:
        for j in range(rows // K_TILE):
            kb_ref[j] = kvt[0:d_attn, j * K_TILE:(j + 1) * K_TILE].astype(BF16)
            vb_ref[j] = kvt[d_attn:2 * d_attn, j * K_TILE:(j + 1) * K_TILE].astype(BF16)

    off = off + 3 * d_attn
    sel_conv = proj(off, d_model)
    br_conv = _dot(y_conv.astype(BF16), wbc_ref[...])
    mconv_ref[...] = (jax.nn.sigmoid(sel_conv) * br_conv).astype(BF16)
    sel_attn = proj(off + d_model, d_model)
    sga_ref[...] = jax.nn.sigmoid(sel_attn).astype(BF16)


def _mix_in_call(x, mod, hist, g, w_in, w_kvt, conv_w, conv_b, w_br_conv, *, slab, d_attn,
                 rowmajor_kv, q_scale):
    nb, rows_total, d_model = x.shape
    d_conv = conv_w.shape[1]
    d_in = w_in.shape[1]
    tm = min(ROW_TILE, rows_total)
    nt = rows_total // tm
    r = mod.shape[2]
    pad = max(V7X_SUBLANES, 2 * slab)

    def row_spec(width):
        return pl.BlockSpec((None, tm, width), lambda b, t: (b, t, 0))

    def out(width, dtype):
        return jax.ShapeDtypeStruct((nb, rows_total, width), dtype)

    col_spec = pl.BlockSpec((None, d_attn, tm), lambda b, t: (b, 0, t))
    col_out = jax.ShapeDtypeStruct((nb, d_attn, rows_total), F32)
    if rowmajor_kv:
        kv_spec, kv_out = row_spec(d_attn), out(d_attn, BF16)
    else:
        kv_spec = pl.BlockSpec((None, tm // K_TILE, d_attn, K_TILE), lambda b, t: (b, t, 0, 0))
        kv_out = jax.ShapeDtypeStruct((nb, rows_total // K_TILE, d_attn, K_TILE), BF16)

    kern = functools.partial(_mix_in_kernel, slab=slab, pad=pad, d_conv=d_conv,
                             d_attn=d_attn, d_model=d_model, rowmajor_kv=rowmajor_kv,
                             q_scale=q_scale)
    return pl.pallas_call(
        kern,
        grid=(nb, nt),
        in_specs=[row_spec(d_model),
                  pl.BlockSpec((None, N_MOD, r, d_model), lambda b, t: (b, 0, 0, 0)),
                  pl.BlockSpec((None, 2 * slab, d_conv), lambda b, t: (b, 0, 0)),
                  _resident((1, d_model), lambda b, t: (0, 0)),
                  _resident((d_model, d_in), lambda b, t: (0, 0)),
                  _resident((2 * d_attn, d_model), lambda b, t: (0, 0)),
                  _resident((3, d_conv), lambda b, t: (0, 0)),
                  _resident((1, d_conv), lambda b, t: (0, 0)),
                  _resident((d_conv, d_model), lambda b, t: (0, 0))],
        out_specs=[col_spec, col_spec, row_spec(d_attn), kv_spec, kv_spec,
                   row_spec(d_model), row_spec(d_model),
                   pl.BlockSpec((None, 2 * slab, d_conv), lambda b, t: (b, 0, 0))],
        out_shape=[col_out, col_out, out(d_attn, BF16), kv_out, kv_out,
                   out(d_model, BF16), out(d_model, BF16),
                   jax.ShapeDtypeStruct((nb, 2 * slab, d_conv), F32)],
        scratch_shapes=[pltpu.VMEM((pad + tm, d_conv), F32)],
        compiler_params=pltpu.CompilerParams(dimension_semantics=("arbitrary", "arbitrary"),
                                             vmem_limit_bytes=V7X_VMEM_LIMIT),
        name="mix_in",
    )(x, mod, hist, g, w_in, w_kvt, conv_w, conv_b, w_br_conv)


def _mix_out_kernel(x_ref, o_ref, mconv_ref, sga_ref, mod_ref, hist_ref, wba_ref, wo_ref,
                    gf_ref, wup_ref, fw_ref, fb_ref, wdn_ref, gfin_ref,
                    y_ref, fst_ref,
                    car_ref, exta_ref, extg_ref, acc_ref, hf_ref,
                    *, slab, pad, d_ff):
    t = pl.program_id(1)
    rows = x_ref.shape[0]

    @pl.when(t == 0)
    def _():
        car_ref[...] = jnp.zeros(car_ref.shape, F32)
        car_ref[pad - 2 * slab:pad, :] = hist_ref[...]

    br_attn = _dot(o_ref[...], wba_ref[...])
    merged = mconv_ref[...].astype(F32) + sga_ref[...].astype(F32) * br_attn
    gate_m = _expand_rows(mod_ref[2], rows)
    x1 = x_ref[...] + gate_m * _dot(merged.astype(BF16), wo_ref[...])

    shift = _expand_rows(mod_ref[3], rows)
    scale = _expand_rows(mod_ref[4], rows)
    hf_ref[...] = (_rms_norm(x1, gf_ref[...]) * (1.0 + scale) + shift).astype(BF16)

    for c in range(d_ff // FFN_CHUNK):
        ca = c * FFN_CHUNK
        cg = d_ff + ca
        hf = hf_ref[...]
        exta_ref[0:pad, :] = car_ref[:, ca:ca + FFN_CHUNK]
        extg_ref[0:pad, :] = car_ref[:, cg:cg + FFN_CHUNK]
        exta_ref[pad:pad + rows, :] = _dot(hf, wup_ref[:, ca:ca + FFN_CHUNK])
        extg_ref[pad:pad + rows, :] = _dot(hf, wup_ref[:, cg:cg + FFN_CHUNK])
        a = _causal_dwconv(exta_ref, pad, slab, rows, fw_ref, fb_ref, ca, FFN_CHUNK)
        g = _causal_dwconv(extg_ref, pad, slab, rows, fw_ref, fb_ref, cg, FFN_CHUNK)
        car_ref[:, ca:ca + FFN_CHUNK] = exta_ref[rows:rows + pad, :]
        car_ref[:, cg:cg + FFN_CHUNK] = extg_ref[rows:rows + pad, :]
        h = (g * jax.nn.sigmoid(g) * a).astype(BF16)
        part = _dot(h, wdn_ref[ca:ca + FFN_CHUNK, :])
        if c == 0:
            acc_ref[...] = part
        else:
            acc_ref[...] += part

    fst_ref[...] = car_ref[pad - 2 * slab:pad, :]
    gate_f = _expand_rows(mod_ref[5], rows)
    x2 = x1 + gate_f * acc_ref[...]
    y_ref[...] = _rms_norm(x2, gfin_ref[...])


def _mix_out_call(x, o_attn, mconv, sga, mod, hist, w_br_attn, w_o, g_ffn, w_up, ffn_w, ffn_b,
                  w_down, g_final, *, slab):
    nb, rows_total, d_model = x.shape
    d_attn = o_attn.shape[2]
    d_ff = w_down.shape[0]
    tm = min(ROW_TILE, rows_total)
    nt = rows_total // tm
    r = mod.shape[2]
    pad = max(V7X_SUBLANES, 2 * slab)

    def row_spec(width):
        return pl.BlockSpec((None, tm, width), lambda b, t: (b, t, 0))

    def const(shape):
        return _resident(shape, lambda b, t: (0,) * len(shape))

    kern = functools.partial(_mix_out_kernel, slab=slab, pad=pad, d_ff=d_ff)
    return pl.pallas_call(
        kern,
        grid=(nb, nt),
        in_specs=[row_spec(d_model), row_spec(d_attn), row_spec(d_model), row_spec(d_model),
                  pl.BlockSpec((None, N_MOD, r, d_model), lambda b, t: (b, 0, 0, 0)),
                  pl.BlockSpec((None, 2 * slab, 2 * d_ff), lambda b, t: (b, 0, 0)),
                  const((d_attn, d_model)), const((d_model, d_model)), const((1, d_model)),
                  const((d_model, 2 * d_ff)), const((3, 2 * d_ff)), const((1, 2 * d_ff)),
                  const((d_ff, d_model)), const((1, d_model))],
        out_specs=[row_spec(d_model),
                   pl.BlockSpec((None, 2 * slab, 2 * d_ff), lambda b, t: (b, 0, 0))],
        out_shape=[jax.ShapeDtypeStruct((nb, rows_total, d_model), F32),
                   jax.ShapeDtypeStruct((nb, 2 * slab, 2 * d_ff), F32)],
        scratch_shapes=[pltpu.VMEM((pad, 2 * d_ff), F32),
                        pltpu.VMEM((pad + tm, FFN_CHUNK), F32),
                        pltpu.VMEM((pad + tm, FFN_CHUNK), F32),
                        pltpu.VMEM((tm, d_model), F32),
                        pltpu.VMEM((tm, d_model), BF16)],
        compiler_params=pltpu.CompilerParams(dimension_semantics=("arbitrary", "arbitrary"),
                                             vmem_limit_bytes=V7X_VMEM_LIMIT),
        name="mix_out_ffn",
    )(x, o_attn, mconv, sga, mod, hist, w_br_attn, w_o, g_ffn, w_up, ffn_w, ffn_b, w_down,
      g_final)


def _sb_block(qh, bias, kblk, vblk, u2, carry, mask, keys_on_lanes):
    z = (_dot(qh, kblk) if keys_on_lanes else _dot_nt(qh, kblk)) + bias
    nlk = _softplus(z)
    if mask is not None:
        nlk = jnp.where(mask, nlk, 0.0)
    hi, lo = _split_bf16(nlk)
    s2 = _dot(jnp.concatenate([hi, lo], axis=1), u2)
    w = jnp.exp(z - s2[:, :K_TILE] - carry)
    if mask is not None:
        w = jnp.where(mask, w, 0.0)
    w = w.astype(BF16)
    pv = _dot_nt(w, vblk) if keys_on_lanes else _dot(w, vblk)
    return pv, carry + s2[:, K_TILE:]


def _softplus2(z):
    return jnp.maximum(LOG2E * jnp.log(1.0 + jnp.exp2(jnp.minimum(z, 126.0))), z)


def _attn_prompt_kernel(bias_ref, q_ref, k_ref, v_ref, u_ref, o_ref,
                        qh_ref, z_ref, s_ref, bm_ref, acc_ref, car_ref):
    hp = pl.program_id(1)
    i = pl.program_id(2)
    tq = q_ref.shape[0]
    nd = tq // UNIT_KEYS
    blocks = UNIT_KEYS // K_TILE
    nunit = (i + 1) * nd
    heads = range(HEADS_PER_LANE_TILE)

    @pl.when(i == 0)
    def _():
        key = lax.broadcasted_iota(jnp.int32, (tq, UNIT_KEYS), 1)
        row = lax.broadcasted_iota(jnp.int32, (tq, UNIT_KEYS), 0)
        for h in heads:
            bias = jnp.full((tq, UNIT_KEYS), bias_ref[HEADS_PER_LANE_TILE * hp + h] * LOG2E, F32)
            for d in range(nd):
                bm_ref[h, nd - 1 - d] = jnp.where(key + d * UNIT_KEYS < row, bias, MASKED_LOGIT)
            bm_ref[h, nd] = bias

    q = q_ref[...]
    lane = lax.broadcasted_iota(jnp.int32, q.shape, 1)
    for h in heads:
        qh_ref[h] = jnp.where((lane >= HEAD_DIM) == (h == 1), q, jnp.zeros_like(q))
    acc_ref[...] = jnp.zeros(acc_ref.shape, F32)
    car_ref[...] = jnp.zeros(car_ref.shape, F32)

    def unit_keys(ref, n):
        first = (nunit - 1 - n) * blocks
        return jnp.concatenate([ref[first + j] for j in range(blocks)], axis=1)

    def logits(n):
        kk = unit_keys(k_ref, n)
        idx = min(n, nd) if isinstance(n, int) else jnp.minimum(n, nd)
        for h in heads:
            z_ref[n & 3, h] = _dot(qh_ref[h], kk) + bm_ref[h, idx]

    def suffix_sums(n):
        for h in heads:
            s_ref[n & 3, h] = _dot(_softplus2(z_ref[n & 3, h]).astype(BF16), u_ref[...])

    def weights_times_v(n):
        vv = unit_keys(v_ref, n)
        top = lax.broadcasted_iota(jnp.int32, vv.shape, 0) < HEAD_DIM
        zero = jnp.zeros_like(vv)
        ws = []
        for h in heads:
            car = car_ref[h]
            t = s_ref[n & 3, h] + jnp.concatenate([car] * blocks, axis=1)
            ws.append(jnp.exp2(z_ref[n & 3, h] - t).astype(BF16))
            car_ref[h] = jnp.broadcast_to(t[:, 0:1], car.shape)
        vd = jnp.concatenate([jnp.where(top, vv, zero), jnp.where(top, zero, vv)], axis=1)
        acc_ref[...] += _dot_nt(jnp.concatenate(ws, axis=1), vd)

    logits(0)
    logits(1)
    suffix_sums(0)

    def trip(n, _):
        weights_times_v(n)
        suffix_sums(n + 1)
        logits(n + 2)
        return 0

    lax.fori_loop(0, nunit - 2, trip, 0)
    weights_times_v(nunit - 2)
    suffix_sums(nunit - 1)
    weights_times_v(nunit - 1)
    o_ref[...] = acc_ref[...].astype(o_ref.dtype)


def _attn_prompt_call(qb, kb, vb, sb_bias, u_pair):
    b, s, d_attn = qb.shape
    tq = min(Q_TILE, s)
    nd = tq // UNIT_KEYS
    assert nd >= 2, "the three-stage sweep needs at least two units per query tile"
    unit = (HEADS_PER_LANE_TILE, tq, UNIT_KEYS)
    grid = (b, d_attn // V7X_LANES, s // tq)
    kv_spec = pl.BlockSpec((None, s // K_TILE, V7X_LANES, K_TILE), lambda bb, hp, i: (bb, 0, hp, 0))
    return pl.pallas_call(
        _attn_prompt_kernel,
        grid=grid,
        in_specs=[pl.BlockSpec(memory_space=pltpu.SMEM),
                  pl.BlockSpec((None, tq, V7X_LANES), lambda bb, hp, i: (bb, i, hp)),
                  kv_spec, kv_spec,
                  _resident(u_pair.shape, lambda bb, hp, i: (0, 0))],
        out_specs=pl.BlockSpec((None, tq, V7X_LANES), lambda bb, hp, i: (bb, i, hp)),
        out_shape=jax.ShapeDtypeStruct((b, s, d_attn), BF16),
        scratch_shapes=[pltpu.VMEM((HEADS_PER_LANE_TILE, tq, V7X_LANES), BF16),
                        pltpu.VMEM((4,) + unit, F32),
                        pltpu.VMEM((4,) + unit, F32),
                        pltpu.VMEM((HEADS_PER_LANE_TILE, nd + 1, tq, UNIT_KEYS), F32),
                        pltpu.VMEM((tq, V7X_LANES), F32),
                        pltpu.VMEM((HEADS_PER_LANE_TILE, tq, K_TILE), F32)],
        compiler_params=pltpu.CompilerParams(
            dimension_semantics=("arbitrary", "arbitrary", "arbitrary"),
            vmem_limit_bytes=V7X_VMEM_LIMIT),
        name="sb_attn_prompt",
    )(sb_bias, qb, kb, vb, u_pair)


def _attn_sample_kernel(pt_ref, q_ref, kn_ref, vn_ref, bias_ref, hmask_ref, nmask_ref, sel_ref,
                        u2_ref, *rest, n_pages_step):
    page_refs = rest[:2 * n_pages_step]
    o_ref, acc_ref, car_ref = rest[2 * n_pages_step:]
    s = pl.program_id(1)
    hmask = hmask_ref[...]
    qbd = q_ref[...] * hmask.astype(BF16)
    bias = bias_ref[...]
    u2 = u2_ref[...]

    def sweep(kblk, vblk, mask, keys_on_lanes):
        pv, car = _sb_block(qbd, bias, kblk, vblk, u2, car_ref[...], mask, keys_on_lanes)
        acc_ref[...] += pv
        car_ref[...] = car

    @pl.when(s == 0)
    def _():
        acc_ref[...] = jnp.zeros(acc_ref.shape, F32)
        car_ref[...] = jnp.zeros(car_ref.shape, F32)
        fill = jnp.zeros((K_TILE - kn_ref.shape[0], kn_ref.shape[1]), BF16)
        sweep(jnp.concatenate([kn_ref[...], fill], axis=0),
              jnp.concatenate([vn_ref[...], fill], axis=0), nmask_ref[...] > 0.0, False)

    zs = [_dot(qbd, page_refs[p][...].astype(BF16)) + bias for p in range(n_pages_step)]
    s2s = []
    for z in zs:
        hi, lo = _split_bf16(_softplus(z))
        s2s.append(_dot(jnp.concatenate([hi, lo], axis=1), u2))
    car = car_ref[...]
    pv = None
    for p in range(n_pages_step):
        w = jnp.exp(zs[p] - s2s[p][:, :K_TILE] - car).astype(BF16)
        part = _dot_nt(w, page_refs[n_pages_step + p][...].astype(BF16))
        pv = part if pv is None else pv + part
        car = car + s2s[p][:, K_TILE:]
    acc_ref[...] += pv
    car_ref[...] = car

    @pl.when(s == pl.num_programs(1) - 1)
    def _():
        hi, lo = _split_bf16(acc_ref[...] * hmask)
        sel = sel_ref[...]
        o_ref[...] = (_dot(sel, hi) + _dot(sel, lo))[0:o_ref.shape[0], :]


def _attn_sample_call(q_exp, k_new, v_new, cache_k, cache_v, page_table, bias_rows, hmask, nmask,
                      sel, u2, t_new):
    bd, rows, d_attn = q_exp.shape
    n_pages = page_table.shape[1]
    page = cache_k.shape[2]
    pps = PAGES_PER_STEP if n_pages % PAGES_PER_STEP == 0 else 1
    steps = n_pages // pps

    def const(a):
        return _resident(a.shape, lambda b, s, pt: (0,) * a.ndim)

    def page_spec(p):
        def index(b, s, pt):
            return (pt[b * n_pages + (n_pages - 1 - (s * pps + p))], 0, 0)
        return pl.BlockSpec((None, d_attn, page), index)

    new_spec = pl.BlockSpec((None, k_new.shape[1], d_attn), lambda b, s, pt: (b, 0, 0))
    kern = functools.partial(_attn_sample_kernel, n_pages_step=pps)
    return pl.pallas_call(
        kern,
        grid_spec=pltpu.PrefetchScalarGridSpec(
            num_scalar_prefetch=1,
            grid=(bd, steps),
            in_specs=[pl.BlockSpec((None, rows, d_attn), lambda b, s, pt: (b, 0, 0)),
                      new_spec, new_spec, const(bias_rows), const(hmask), const(nmask),
                      const(sel), const(u2)]
                     + [page_spec(p) for p in range(pps)] * 2,
            out_specs=pl.BlockSpec((None, t_new, d_attn), lambda b, s, pt: (b, 0, 0)),
            scratch_shapes=[pltpu.VMEM((rows, d_attn), F32), pltpu.VMEM((rows, K_TILE), F32)]),
        out_shape=jax.ShapeDtypeStruct((bd, t_new, d_attn), F32),
        compiler_params=pltpu.CompilerParams(dimension_semantics=("arbitrary", "arbitrary"),
                                             vmem_limit_bytes=V7X_VMEM_LIMIT),
        name="sb_attn_sample",
    )(page_table.reshape(-1), q_exp, k_new, v_new, bias_rows, hmask, nmask, sel, u2,
      *([cache_k] * pps), *([cache_v] * pps))


def kernel(x_prompt, x_sample, cache_k, cache_v, state_conv, state_ffn, page_table, c_prompt, c_sample, w_ada, b_ada, norm_mix_g, w_in, sb_bias, conv_w, conv_b, w_br_conv, w_br_attn, w_o, norm_ffn_g, w_up, ffn_conv_w, ffn_conv_b, w_down, norm_final_g):
    assert w_ada.shape[0] == 1, "single-layer step"
    b, s, d_model = x_prompt.shape
    bd, t_new, _ = x_sample.shape
    n_heads = sb_bias.shape[1]
    d_attn = n_heads * HEAD_DIM
    d_conv = conv_w.shape[2]
    d_ff2 = w_up.shape[2]
    page = cache_k.shape[2]
    assert page == K_TILE and d_attn % V7X_LANES == 0

    w_ada_b, w_in_b, w_brc_b, w_bra_b, w_o_b, w_up_b, w_dn_b = (
        w[0].astype(BF16) for w in (w_ada, w_in, w_br_conv, w_br_attn, w_o, w_up, w_down))
    g_mix, g_ffn, g_fin = norm_mix_g[0][None], norm_ffn_g[0][None], norm_final_g[None]
    cw, cb = conv_w[0], conv_b[0][None]
    fw, fb = ffn_conv_w[0], ffn_conv_b[0][None]
    bias = sb_bias[0]
    u2 = _suffix_matrix(K_TILE)

    mod = _ada_call(jnp.concatenate([c_prompt, c_sample], axis=0), w_ada_b, b_ada[0][None])
    mod_p = mod[:b].reshape(b, N_MOD, 1, d_model)
    mod_s = mod[b:].reshape(1, bd, N_MOD, d_model).transpose(0, 2, 1, 3)

    kv0 = 3 * d_conv + d_attn
    w_kvt = w_in_b[:, kv0:kv0 + 2 * d_attn].T

    ktf, vtf, qb, kb, vb, mconv, sga, cst_p = _mix_in_call(
        x_prompt, mod_p, jnp.zeros((b, 2, d_conv), F32), g_mix, w_in_b, w_kvt, cw, cb, w_brc_b,
        slab=1, d_attn=d_attn, rowmajor_kv=False, q_scale=HEAD_DIM ** -0.5 * LOG2E)
    o_p = _attn_prompt_call(qb, kb, vb, bias, _inclusive_suffix_matrix(UNIT_KEYS))
    y_p, fst_p = _mix_out_call(x_prompt, o_p, mconv, sga, mod_p, jnp.zeros((b, 2, d_ff2), F32),
                               w_bra_b, w_o_b, g_ffn, w_up_b, fw, fb, w_dn_b, g_fin, slab=1)

    def time_major(a):
        return a.transpose(1, 0, 2).reshape(1, a.shape[1] * bd, a.shape[2])

    def batch_major(a, steps):
        return a.reshape(steps, bd, a.shape[-1]).transpose(1, 0, 2)

    xs = time_major(x_sample)
    ktf_s, vtf_s, qb_s, kb_s, vb_s, mconv_s, sga_s, cst_s = _mix_in_call(
        xs, mod_s, time_major(state_conv[0]), g_mix, w_in_b, w_kvt, cw, cb, w_brc_b,
        slab=bd, d_attn=d_attn, rowmajor_kv=True, q_scale=HEAD_DIM ** -0.5)

    rows = n_heads * t_new
    q_exp = jnp.tile(batch_major(qb_s, t_new)[:, None], (1, n_heads, 1, 1)).reshape(bd, rows, d_attn)
    new_pad = 16 - t_new
    k_new = jnp.pad(batch_major(kb_s, t_new), ((0, 0), (0, new_pad), (0, 0)))
    v_new = jnp.pad(batch_major(vb_s, t_new), ((0, 0), (0, new_pad), (0, 0)))
    r_head = jnp.arange(rows) // t_new
    r_time = jnp.arange(rows) % t_new
    bias_rows = jnp.broadcast_to(bias[r_head][:, None], (rows, K_TILE)).astype(F32)
    hmask = (jnp.arange(d_attn)[None, :] // HEAD_DIM == r_head[:, None]).astype(F32)
    nmask = (jnp.arange(K_TILE)[None, :] < r_time[:, None]).astype(F32)
    sel = (jnp.arange(V7X_SUBLANES)[:, None] == r_time[None, :]).astype(BF16)
    def page_layout(cache):
        return cache[0].transpose(0, 2, 3, 1).reshape(-1, d_attn, page)

    o_s = _attn_sample_call(q_exp, k_new, v_new, page_layout(cache_k), page_layout(cache_v),
                            page_table, bias_rows, hmask, nmask, sel, u2, t_new)
    y_s, fst_s = _mix_out_call(xs, time_major(o_s.astype(BF16)), mconv_s, sga_s, mod_s,
                               time_major(state_ffn[0]), w_bra_b, w_o_b, g_ffn, w_up_b, fw, fb,
                               w_dn_b, g_fin, slab=bd)

    k_p =ktf.reshape(b, n_heads, HEAD_DIM, s).transpose(0, 3, 1, 2)[None]
    v_p = vtf.reshape(b, n_heads, HEAD_DIM, s).transpose(0, 3, 1, 2)[None]
    k_s = ktf_s.reshape(n_heads, HEAD_DIM, t_new, bd).transpose(3, 2, 0, 1)[None]
    v_s = vtf_s.reshape(n_heads, HEAD_DIM, t_new, bd).transpose(3, 2, 0, 1)[None]
    return (y_p, batch_major(y_s[0], t_new), k_p, v_p, cst_p[None], fst_p[None], k_s, v_s,
            batch_major(cst_s[0], 2)[None], batch_major(fst_s[0], 2)[None])
```

```python
import functools

import jax
import jax.numpy as jnp
from jax import lax
from jax.experimental import pallas as pl
from jax.experimental.pallas import tpu as pltpu

F32 = jnp.float32
BF16 = jnp.bfloat16

EPS = 1e-6
N_MOD = 6
HEAD_DIM = 64
HEADS_PER_LANE_TILE = 2
V7X_LANES = 128
V7X_SUBLANES = 8
V7X_VMEM_LIMIT = 56 * 1024 * 1024

ROW_TILE = 512
FFN_CHUNK = 256
Q_TILE = 512
UNIT_KEYS = 256
MASKED_LOGIT = -1e30
LOG2E = 1.4426950408889634
K_TILE = 128
PAGES_PER_STEP = 32


def _resident(block_shape, index_map):
    return pl.BlockSpec(block_shape, index_map, pipeline_mode=pl.Buffered(1))


def _expand_rows(m, rows):
    r = m.shape[0]
    if r == 1 or r == rows:
        return m
    return jnp.concatenate([m] * (rows // r), axis=0)


def _rms_norm(x, g):
    ms = jnp.mean(x * x, axis=-1, keepdims=True)
    return x * lax.rsqrt(ms + EPS) * g


def _softplus(z):
    return jnp.maximum(z, 0.0) + jnp.log(1.0 + jnp.exp(-jnp.abs(z)))


def _split_bf16(x):
    hi = x.astype(BF16)
    lo = (x - hi.astype(F32)).astype(BF16)
    return hi, lo


def _dot(a, b):
    return jnp.dot(a, b, preferred_element_type=F32)


def _dot_nt(a, b):
    return lax.dot_general(a, b, (((1,), (1,)), ((), ())), preferred_element_type=F32)


def _inclusive_suffix_matrix(keys):
    j = jnp.arange(keys)[:, None]
    s = jnp.arange(keys)[None, :]
    return (j >= s).astype(BF16)


def _suffix_matrix(k_tile):
    j = jnp.arange(k_tile)[:, None]
    s = jnp.arange(k_tile)[None, :]
    u = (j >= s).astype(BF16)
    half = jnp.concatenate([u, jnp.ones((k_tile, k_tile), BF16)], axis=1)
    return jnp.concatenate([half, half], axis=0)


def _causal_dwconv(ext_ref, pad, slab, rows, w_ref, b_ref, col0, width):
    cols = pl.ds(col0, width)
    y = b_ref[:, cols] + w_ref[2:3, cols] * ext_ref[pad:pad + rows, :]
    y = y + w_ref[1:2, cols] * ext_ref[pad - slab:pad - slab + rows, :]
    y = y + w_ref[0:1, cols] * ext_ref[pad - 2 * slab:pad - 2 * slab + rows, :]
    return y


def _ada_kernel(c_ref, w_ref, b_ref, o_ref):
    c = c_ref[...]
    ca = (c * jax.nn.sigmoid(c)).astype(BF16)
    o_ref[...] = _dot(ca, w_ref[...]) + b_ref[...]


def _ada_call(c_all, w_ada, b_ada):
    n, d = c_all.shape
    dn = w_ada.shape[1]
    tn = d
    return pl.pallas_call(
        _ada_kernel,
        grid=(dn // tn,),
        in_specs=[pl.BlockSpec((n, d), lambda j: (0, 0)),
                  pl.BlockSpec((d, tn), lambda j: (0, j)),
                  pl.BlockSpec((1, tn), lambda j: (0, j))],
        out_specs=pl.BlockSpec((n, tn), lambda j: (0, j)),
        out_shape=jax.ShapeDtypeStruct((n, dn), F32),
        compiler_params=pltpu.CompilerParams(dimension_semantics=("arbitrary",),
                                             vmem_limit_bytes=V7X_VMEM_LIMIT),
        name="ada_mod",
    )(c_all, w_ada, b_ada)


def _mix_in_kernel(x_ref, mod_ref, hist_ref, g_ref, w_in_ref, wkvt_ref, cw_ref, cb_ref, wbc_ref,
                   ktf_ref, vtf_ref, qb_ref, kb_ref, vb_ref, mconv_ref, sga_ref, cst_ref,
                   ext_ref, *, slab, pad, d_conv, d_attn, d_model, rowmajor_kv, q_scale):
    t = pl.program_id(1)
    rows = x_ref.shape[0]

    @pl.when(t == 0)
    def _():
        ext_ref[0:pad, :] = jnp.zeros((pad, d_conv), F32)
        ext_ref[pad - 2 * slab:pad, :] = hist_ref[...]

    shift = _expand_rows(mod_ref[0], rows)
    scale = _expand_rows(mod_ref[1], rows)
    hm = _rms_norm(x_ref[...], g_ref[...]) * (1.0 + scale) + shift
    hb = hm.astype(BF16)

    def proj(c0, width):
        return _dot(hb, w_in_ref[:, c0:c0 + width])

    gate_b = proj(0, d_conv)
    gate_c = proj(d_conv, d_conv)
    u_in = proj(2 * d_conv, d_conv)
    ext_ref[pad:pad + rows, :] = gate_c * u_in
    y_conv = gate_b * _causal_dwconv(ext_ref, pad, slab, rows, cw_ref, cb_ref, 0, d_conv)
    cst_ref[...] = ext_ref[pad + rows - 2 * slab:pad + rows, :]
    ext_ref[0:pad, :] = ext_ref[rows:rows + pad, :]

    off = 3 * d_conv
    q = proj(off, d_attn)
    qb_ref[...] = (q * q_scale).astype(BF16)
    kvt = _dot_nt(wkvt_ref[...], hb)
    ktf_ref[...] = kvt[0:d_attn]
    vtf_ref[...] = kvt[d_attn:2 * d_attn]
    if rowmajor_kv:
        kb_ref[...] = proj(off + d_attn, d_attn).astype(BF16)
        vb_ref[...] = proj(off + 2 * d_attn, d_attn).astype(BF16)
    else:
        for j in range(rows // K_TILE):
            kb_ref[j] = kvt[0:d_attn, j * K_TILE:(j + 1) * K_TILE].astype(BF16)
            vb_ref[j] = kvt[d_attn:2 * d_attn, j * K_TILE:(j + 1) * K_TILE].astype(BF16)

    off = off + 3 * d_attn
    sel_conv = proj(off, d_model)
    br_conv = _dot(y_conv.astype(BF16), wbc_ref[...])
    mconv_ref[...] = (jax.nn.sigmoid(sel_conv) * br_conv).astype(BF16)
    sel_attn = proj(off + d_model, d_model)
    sga_ref[...] = jax.nn.sigmoid(sel_attn).astype(BF16)


def _mix_in_call(x, mod, hist, g, w_in, w_kvt, conv_w, conv_b, w_br_conv, *, slab, d_attn,
                 rowmajor_kv, q_scale):
    nb, rows_total, d_model = x.shape
    d_conv = conv_w.shape[1]
    d_in = w_in.shape[1]
    tm = min(ROW_TILE, rows_total)
    nt = rows_total // tm
    r = mod.shape[2]
    pad = max(V7X_SUBLANES, 2 * slab)

    def row_spec(width):
        return pl.BlockSpec((None, tm, width), lambda b, t: (b, t, 0))

    def out(width, dtype):
        return jax.ShapeDtypeStruct((nb, rows_total, width), dtype)

    col_spec = pl.BlockSpec((None, d_attn, tm), lambda b, t: (b, 0, t))
    col_out = jax.ShapeDtypeStruct((nb, d_attn, rows_total), F32)
    if rowmajor_kv:
        kv_spec, kv_out = row_spec(d_attn), out(d_attn, BF16)
    else:
        kv_spec = pl.BlockSpec((None, tm // K_TILE, d_attn, K_TILE), lambda b, t: (b, t, 0, 0))
        kv_out = jax.ShapeDtypeStruct((nb, rows_total // K_TILE, d_attn, K_TILE), BF16)

    kern = functools.partial(_mix_in_kernel, slab=slab, pad=pad, d_conv=d_conv,
                             d_attn=d_attn, d_model=d_model, rowmajor_kv=rowmajor_kv,
                             q_scale=q_scale)
    return pl.pallas_call(
        kern,
        grid=(nb, nt),
        in_specs=[row_spec(d_model),
                  pl.BlockSpec((None, N_MOD, r, d_model), lambda b, t: (b, 0, 0, 0)),
                  pl.BlockSpec((None, 2 * slab, d_conv), lambda b, t: (b, 0, 0)),
                  _resident((1, d_model), lambda b, t: (0, 0)),
                  _resident((d_model, d_in), lambda b, t: (0, 0)),
                  _resident((2 * d_attn, d_model), lambda b, t: (0, 0)),
                  _resident((3, d_conv), lambda b, t: (0, 0)),
                  _resident((1, d_conv), lambda b, t: (0, 0)),
                  _resident((d_conv, d_model), lambda b, t: (0, 0))],
        out_specs=[col_spec, col_spec, row_spec(d_attn), kv_spec, kv_spec,
                   row_spec(d_model), row_spec(d_model),
                   pl.BlockSpec((None, 2 * slab, d_conv), lambda b, t: (b, 0, 0))],
        out_shape=[col_out, col_out, out(d_attn, BF16), kv_out, kv_out,
                   out(d_model, BF16), out(d_model, BF16),
                   jax.ShapeDtypeStruct((nb, 2 * slab, d_conv), F32)],
        scratch_shapes=[pltpu.VMEM((pad + tm, d_conv), F32)],
        compiler_params=pltpu.CompilerParams(dimension_semantics=("arbitrary", "arbitrary"),
                                             vmem_limit_bytes=V7X_VMEM_LIMIT),
        name="mix_in",
    )(x, mod, hist, g, w_in, w_kvt, conv_w, conv_b, w_br_conv)


def _mix_out_kernel(x_ref, o_ref, mconv_ref, sga_ref, mod_ref, hist_ref, wba_ref, wo_ref,
                    gf_ref, wup_ref, fw_ref, fb_ref, wdn_ref, gfin_ref,
                    y_ref, fst_ref,
                    car_ref, exta_ref, extg_ref, x1_ref, hf_ref, h_ref,
                    *, slab, pad, d_ff):
    t = pl.program_id(1)
    rows = x_ref.shape[0]

    @pl.when(t == 0)
    def _():
        car_ref[...] = jnp.zeros(car_ref.shape, F32)
        car_ref[pad - 2 * slab:pad, :] = hist_ref[...]

    br_attn = _dot(o_ref[...], wba_ref[...])
    merged = mconv_ref[...].astype(F32) + sga_ref[...].astype(F32) * br_attn
    gate_m = _expand_rows(mod_ref[2], rows)
    x1 = x_ref[...] + gate_m * _dot(merged.astype(BF16), wo_ref[...])
    x1_ref[...] = x1

    shift = _expand_rows(mod_ref[3], rows)
    scale = _expand_rows(mod_ref[4], rows)
    hf_ref[...] = (_rms_norm(x1, gf_ref[...]) * (1.0 + scale) + shift).astype(BF16)

    for c in range(d_ff // FFN_CHUNK):
        ca = c * FFN_CHUNK
        cg = d_ff + ca
        hf = hf_ref[...]
        exta = exta_ref.at[c % 2]
        extg = extg_ref.at[c % 2]
        exta[0:pad, :] = car_ref[:, ca:ca + FFN_CHUNK]
        extg[0:pad, :] = car_ref[:, cg:cg + FFN_CHUNK]
        exta[pad:pad + rows, :] = _dot(hf, wup_ref[:, ca:ca + FFN_CHUNK])
        extg[pad:pad + rows, :] = _dot(hf, wup_ref[:, cg:cg + FFN_CHUNK])
        a = _causal_dwconv(exta, pad, slab, rows, fw_ref, fb_ref, ca, FFN_CHUNK)
        g = _causal_dwconv(extg, pad, slab, rows, fw_ref, fb_ref, cg, FFN_CHUNK)
        car_ref[:, ca:ca + FFN_CHUNK] = exta[rows:rows + pad, :]
        car_ref[:, cg:cg + FFN_CHUNK] = extg[rows:rows + pad, :]
        h_ref[:, ca:ca + FFN_CHUNK] = (g * jax.nn.sigmoid(g) * a).astype(BF16)

    fst_ref[...] = car_ref[pad - 2 * slab:pad, :]
    gate_f = _expand_rows(mod_ref[5], rows)
    x2 = x1_ref[...] + gate_f * _dot(h_ref[...], wdn_ref[...])
    y_ref[...] = _rms_norm(x2, gfin_ref[...])


def _mix_out_call(x, o_attn, mconv, sga, mod, hist, w_br_attn, w_o, g_ffn, w_up, ffn_w, ffn_b,
                  w_down, g_final, *, slab):
    nb, rows_total, d_model = x.shape
    d_attn = o_attn.shape[2]
    d_ff = w_down.shape[0]
    tm = min(ROW_TILE, rows_total)
    nt = rows_total // tm
    r = mod.shape[2]
    pad = max(V7X_SUBLANES, 2 * slab)

    def row_spec(width):
        return pl.BlockSpec((None, tm, width), lambda b, t: (b, t, 0))

    def const(shape):
        return _resident(shape, lambda b, t: (0,) * len(shape))

    kern = functools.partial(_mix_out_kernel, slab=slab, pad=pad, d_ff=d_ff)
    return pl.pallas_call(
        kern,
        grid=(nb, nt),
        in_specs=[row_spec(d_model), row_spec(d_attn), row_spec(d_model), row_spec(d_model),
                  pl.BlockSpec((None, N_MOD, r, d_model), lambda b, t: (b, 0, 0, 0)),
                  pl.BlockSpec((None, 2 * slab, 2 * d_ff), lambda b, t: (b, 0, 0)),
                  const((d_attn, d_model)), const((d_model, d_model)), const((1, d_model)),
                  const((d_model, 2 * d_ff)), const((3, 2 * d_ff)), const((1, 2 * d_ff)),
                  const((d_ff, d_model)), const((1, d_model))],
        out_specs=[row_spec(d_model),
                   pl.BlockSpec((None, 2 * slab, 2 * d_ff), lambda b, t: (b, 0, 0))],
        out_shape=[jax.ShapeDtypeStruct((nb, rows_total, d_model), F32),
                   jax.ShapeDtypeStruct((nb, 2 * slab, 2 * d_ff), F32)],
        scratch_shapes=[pltpu.VMEM((pad, 2 * d_ff), F32),
                        pltpu.VMEM((2, pad + tm, FFN_CHUNK), F32),
                        pltpu.VMEM((2, pad + tm, FFN_CHUNK), F32),
                        pltpu.VMEM((tm, d_model), F32),
                        pltpu.VMEM((tm, d_model), BF16),
                        pltpu.VMEM((tm, d_ff), BF16)],
        compiler_params=pltpu.CompilerParams(dimension_semantics=("arbitrary", "arbitrary"),
                                             vmem_limit_bytes=V7X_VMEM_LIMIT),
        name="mix_out_ffn",
    )(x, o_attn, mconv, sga, mod, hist, w_br_attn, w_o, g_ffn, w_up, ffn_w, ffn_b, w_down,
      g_final)


def _sb_block(qh, bias, kblk, vblk, u2, carry, mask, keys_on_lanes):
    z = (_dot(qh, kblk) if keys_on_lanes else _dot_nt(qh, kblk)) + bias
    nlk = _softplus(z)
    if mask is not None:
        nlk = jnp.where(mask, nlk, 0.0)
    hi, lo = _split_bf16(nlk)
    s2 = _dot(jnp.concatenate([hi, lo], axis=1), u2)
    w = jnp.exp(z - s2[:, :K_TILE] - carry)
    if mask is not None:
        w = jnp.where(mask, w, 0.0)
    w = w.astype(BF16)
    pv = _dot_nt(w, vblk) if keys_on_lanes else _dot(w, vblk)
    return pv, carry + s2[:, K_TILE:]


def _softplus2(z):
    return jnp.maximum(LOG2E * jnp.log(1.0 + jnp.exp2(jnp.minimum(z, 126.0))), z)


def _attn_prompt_kernel(bias_ref, q_ref, k_ref, v_ref, u_ref, o_ref,
                        qh_ref, z_ref, s_ref, bm_ref, acc_ref, car_ref):
    hp = pl.program_id(1)
    i = pl.program_id(2)
    tq = q_ref.shape[0]
    nd = tq // UNIT_KEYS
    blocks = UNIT_KEYS // K_TILE
    nunit = (i + 1) * nd
    heads = range(HEADS_PER_LANE_TILE)

    @pl.when(i == 0)
    def _():
        key = lax.broadcasted_iota(jnp.int32, (tq, UNIT_KEYS), 1)
        row = lax.broadcasted_iota(jnp.int32, (tq, UNIT_KEYS), 0)
        for h in heads:
            bias = jnp.full((tq, UNIT_KEYS), bias_ref[HEADS_PER_LANE_TILE * hp + h] * LOG2E, F32)
            for d in range(nd):
                bm_ref[h, nd - 1 - d] = jnp.where(key + d * UNIT_KEYS < row, bias, MASKED_LOGIT)
            bm_ref[h, nd] = bias

    q = q_ref[...]
    lane = lax.broadcasted_iota(jnp.int32, q.shape, 1)
    for h in heads:
        qh_ref[h] = jnp.where((lane >= HEAD_DIM) == (h == 1), q, jnp.zeros_like(q))
    acc_ref[...] = jnp.zeros(acc_ref.shape, F32)
    car_ref[...] = jnp.zeros(car_ref.shape, F32)

    def unit_keys(ref, n):
        first = (nunit - 1 - n) * blocks
        return jnp.concatenate([ref[first + j] for j in range(blocks)], axis=1)

    def logits(n):
        kk = unit_keys(k_ref, n)
        idx = min(n, nd) if isinstance(n, int) else jnp.minimum(n, nd)
        for h in heads:
            z_ref[n & 3, h] = _dot(qh_ref[h], kk) + bm_ref[h, idx]

    def suffix_sums(n):
        for h in heads:
            s_ref[n & 3, h] = _dot(_softplus2(z_ref[n & 3, h]).astype(BF16), u_ref[...])

    def weights_times_v(n):
        vv = unit_keys(v_ref, n)
        top = lax.broadcasted_iota(jnp.int32, vv.shape, 0) < HEAD_DIM
        zero = jnp.zeros_like(vv)
        ws = []
        for h in heads:
            car = car_ref[h]
            t = s_ref[n & 3, h] + jnp.concatenate([car] * blocks, axis=1)
            ws.append(jnp.exp2(z_ref[n & 3, h] - t).astype(BF16))
            car_ref[h] = jnp.broadcast_to(t[:, 0:1], car.shape)
        vd = jnp.concatenate([jnp.where(top, vv, zero), jnp.where(top, zero, vv)], axis=1)
        acc_ref[...] += _dot_nt(jnp.concatenate(ws, axis=1), vd)

    logits(0)
    logits(1)
    suffix_sums(0)

    def trip(n, _):
        weights_times_v(n)
        suffix_sums(n + 1)
        logits(n + 2)
        return 0

    lax.fori_loop(0, nunit - 2, trip, 0)
    weights_times_v(nunit - 2)
    suffix_sums(nunit - 1)
    weights_times_v(nunit - 1)
    o_ref[...] = acc_ref[...].astype(o_ref.dtype)


def _attn_prompt_call(qb, kb, vb, sb_bias, u_pair):
    b, s, d_attn = qb.shape
    tq = min(Q_TILE, s)
    nd = tq // UNIT_KEYS
    assert nd >= 2, "the three-stage sweep needs at least two units per query tile"
    unit = (HEADS_PER_LANE_TILE, tq, UNIT_KEYS)
    grid = (b, d_attn // V7X_LANES, s // tq)
    kv_spec = pl.BlockSpec((None, s // K_TILE, V7X_LANES, K_TILE), lambda bb, hp, i: (bb, 0, hp, 0))
    return pl.pallas_call(
        _attn_prompt_kernel,
        grid=grid,
        in_specs=[pl.BlockSpec(memory_space=pltpu.SMEM),
                  pl.BlockSpec((None, tq, V7X_LANES), lambda bb, hp, i: (bb, i, hp)),
                  kv_spec, kv_spec,
                  _resident(u_pair.shape, lambda bb, hp, i: (0, 0))],
        out_specs=pl.BlockSpec((None, tq, V7X_LANES), lambda bb, hp, i: (bb, i, hp)),
        out_shape=jax.ShapeDtypeStruct((b, s, d_attn), BF16),
        scratch_shapes=[pltpu.VMEM((HEADS_PER_LANE_TILE, tq, V7X_LANES), BF16),
                        pltpu.VMEM((4,) + unit, F32),
                        pltpu.VMEM((4,) + unit, F32),
                        pltpu.VMEM((HEADS_PER_LANE_TILE, nd + 1, tq, UNIT_KEYS), F32),
                        pltpu.VMEM((tq, V7X_LANES), F32),
                        pltpu.VMEM((HEADS_PER_LANE_TILE, tq, K_TILE), F32)],
        compiler_params=pltpu.CompilerParams(
            dimension_semantics=("arbitrary", "arbitrary", "arbitrary"),
            vmem_limit_bytes=V7X_VMEM_LIMIT),
        name="sb_attn_prompt",
    )(sb_bias, qb, kb, vb, u_pair)


def _attn_sample_kernel(pt_ref, q_ref, kn_ref, vn_ref, bias_ref, hmask_ref, nmask_ref, sel_ref,
                        u2_ref, *rest, n_pages_step):
    page_refs = rest[:2 * n_pages_step]
    o_ref, acc_ref, car_ref = rest[2 * n_pages_step:]
    s = pl.program_id(1)
    hmask = hmask_ref[...]
    qbd = q_ref[...] * hmask.astype(BF16)
    bias = bias_ref[...]
    u2 = u2_ref[...]

    def sweep(kblk, vblk, mask, keys_on_lanes):
        pv, car = _sb_block(qbd, bias, kblk, vblk, u2, car_ref[...], mask, keys_on_lanes)
        acc_ref[...] += pv
        car_ref[...] = car

    @pl.when(s == 0)
    def _():
        acc_ref[...] = jnp.zeros(acc_ref.shape, F32)
        car_ref[...] = jnp.zeros(car_ref.shape, F32)
        fill = jnp.zeros((K_TILE - kn_ref.shape[0], kn_ref.shape[1]), BF16)
        sweep(jnp.concatenate([kn_ref[...], fill], axis=0),
              jnp.concatenate([vn_ref[...], fill], axis=0), nmask_ref[...] > 0.0, False)

    zs = [_dot(qbd, page_refs[p][...].astype(BF16)) + bias for p in range(n_pages_step)]
    s2s = []
    for z in zs:
        hi, lo = _split_bf16(_softplus(z))
        s2s.append(_dot(jnp.concatenate([hi, lo], axis=1), u2))
    car = car_ref[...]
    pv = None
    for p in range(n_pages_step):
        w = jnp.exp(zs[p] - s2s[p][:, :K_TILE] - car).astype(BF16)
        part = _dot_nt(w, page_refs[n_pages_step + p][...].astype(BF16))
        pv = part if pv is None else pv + part
        car = car + s2s[p][:, K_TILE:]
    acc_ref[...] += pv
    car_ref[...] = car

    @pl.when(s == pl.num_programs(1) - 1)
    def _():
        hi, lo = _split_bf16(acc_ref[...] * hmask)
        sel = sel_ref[...]
        o_ref[...] = (_dot(sel, hi) + _dot(sel, lo))[0:o_ref.shape[0], :]


def _attn_sample_call(q_exp, k_new, v_new, cache_k, cache_v, page_table, bias_rows, hmask, nmask,
                      sel, u2, t_new):
    bd, rows, d_attn = q_exp.shape
    n_pages = page_table.shape[1]
    page = cache_k.shape[2]
    pps = PAGES_PER_STEP if n_pages % PAGES_PER_STEP == 0 else 1
    steps = n_pages // pps

    def const(a):
        return _resident(a.shape, lambda b, s, pt: (0,) * a.ndim)

    def page_spec(p):
        def index(b, s, pt):
            return (pt[b * n_pages + (n_pages - 1 - (s * pps + p))], 0, 0)
        return pl.BlockSpec((None, d_attn, page), index)

    new_spec = pl.BlockSpec((None, k_new.shape[1], d_attn), lambda b, s, pt: (b, 0, 0))
    kern = functools.partial(_attn_sample_kernel, n_pages_step=pps)
    return pl.pallas_call(
        kern,
        grid_spec=pltpu.PrefetchScalarGridSpec(
            num_scalar_prefetch=1,
            grid=(bd, steps),
            in_specs=[pl.BlockSpec((None, rows, d_attn), lambda b, s, pt: (b, 0, 0)),
                      new_spec, new_spec, const(bias_rows), const(hmask), const(nmask),
                      const(sel), const(u2)]
                     + [page_spec(p) for p in range(pps)] * 2,
            out_specs=pl.BlockSpec((None, t_new, d_attn), lambda b, s, pt: (b, 0, 0)),
            scratch_shapes=[pltpu.VMEM((rows, d_attn), F32), pltpu.VMEM((rows, K_TILE), F32)]),
        out_shape=jax.ShapeDtypeStruct((bd, t_new, d_attn), F32),
        compiler_params=pltpu.CompilerParams(dimension_semantics=("arbitrary", "arbitrary"),
                                             vmem_limit_bytes=V7X_VMEM_LIMIT),
        name="sb_attn_sample",
    )(page_table.reshape(-1), q_exp, k_new, v_new, bias_rows, hmask, nmask, sel, u2,
      *([cache_k] * pps), *([cache_v] * pps))


def kernel(x_prompt, x_sample, cache_k, cache_v, state_conv, state_ffn, page_table, c_prompt, c_sample, w_ada, b_ada, norm_mix_g, w_in, sb_bias, conv_w, conv_b, w_br_conv, w_br_attn, w_o, norm_ffn_g, w_up, ffn_conv_w, ffn_conv_b, w_down, norm_final_g):
    assert w_ada.shape[0] == 1, "single-layer step"
    b, s, d_model = x_prompt.shape
    bd, t_new, _ = x_sample.shape
    n_heads = sb_bias.shape[1]
    d_attn = n_heads * HEAD_DIM
    d_conv = conv_w.shape[2]
    d_ff2 = w_up.shape[2]
    page = cache_k.shape[2]
    assert page == K_TILE and d_attn % V7X_LANES == 0

    w_ada_b, w_in_b, w_brc_b, w_bra_b, w_o_b, w_up_b, w_dn_b = (
        w[0].astype(BF16) for w in (w_ada, w_in, w_br_conv, w_br_attn, w_o, w_up, w_down))
    g_mix, g_ffn, g_fin = norm_mix_g[0][None], norm_ffn_g[0][None], norm_final_g[None]
    cw, cb = conv_w[0], conv_b[0][None]
    fw, fb = ffn_conv_w[0], ffn_conv_b[0][None]
    bias = sb_bias[0]
    u2 = _suffix_matrix(K_TILE)

    mod = _ada_call(jnp.concatenate([c_prompt, c_sample], axis=0), w_ada_b, b_ada[0][None])
    mod_p = mod[:b].reshape(b, N_MOD, 1, d_model)
    mod_s = mod[b:].reshape(1, bd, N_MOD, d_model).transpose(0, 2, 1, 3)

    kv0 = 3 * d_conv + d_attn
    w_kvt = w_in[0][:, kv0:kv0 + 2 * d_attn].T.astype(BF16)

    ktf, vtf, qb, kb, vb, mconv, sga, cst_p = _mix_in_call(
        x_prompt, mod_p, jnp.zeros((b, 2, d_conv), F32), g_mix, w_in_b, w_kvt, cw, cb, w_brc_b,
        slab=1, d_attn=d_attn, rowmajor_kv=False, q_scale=HEAD_DIM ** -0.5 * LOG2E)
    o_p = _attn_prompt_call(qb, kb, vb, bias, _inclusive_suffix_matrix(UNIT_KEYS))
    y_p, fst_p = _mix_out_call(x_prompt, o_p, mconv, sga, mod_p, jnp.zeros((b, 2, d_ff2), F32),
                               w_bra_b, w_o_b, g_ffn, w_up_b, fw, fb, w_dn_b, g_fin, slab=1)

    def time_major(a):
        return a.transpose(1, 0, 2).reshape(1, a.shape[1] * bd, a.shape[2])

    def batch_major(a, steps):
        return a.reshape(steps, bd, a.shape[-1]).transpose(1, 0, 2)

    xs = time_major(x_sample)
    ktf_s, vtf_s, qb_s, kb_s, vb_s, mconv_s, sga_s, cst_s = _mix_in_call(
        xs, mod_s, time_major(state_conv[0]), g_mix, w_in_b, w_kvt, cw, cb, w_brc_b,
        slab=bd, d_attn=d_attn, rowmajor_kv=True, q_scale=HEAD_DIM ** -0.5)

    rows = n_heads * t_new
    q_exp = jnp.tile(batch_major(qb_s, t_new)[:, None], (1, n_heads, 1, 1)).reshape(bd, rows, d_attn)
    new_pad = 16 - t_new
    k_new = jnp.pad(batch_major(kb_s, t_new), ((0, 0), (0, new_pad), (0, 0)))
    v_new = jnp.pad(batch_major(vb_s, t_new), ((0, 0), (0, new_pad), (0, 0)))
    r_head = jnp.arange(rows) // t_new
    r_time = jnp.arange(rows) % t_new
    bias_rows = jnp.broadcast_to(bias[r_head][:, None], (rows, K_TILE)).astype(F32)
    hmask = (jnp.arange(d_attn)[None, :] // HEAD_DIM == r_head[:, None]).astype(F32)
    nmask = (jnp.arange(K_TILE)[None, :] < r_time[:, None]).astype(F32)
    sel = (jnp.arange(V7X_SUBLANES)[:, None] == r_time[None, :]).astype(BF16)
    def page_layout(cache):
        return cache[0].transpose(0, 2, 3, 1).reshape(-1, d_attn, page)

    o_s = _attn_sample_call(q_exp, k_new, v_new, page_layout(cache_k), page_layout(cache_v),
                            page_table, bias_rows, hmask, nmask, sel, u2, t_new)
    y_s, fst_s = _mix_out_call(xs, time_major(o_s.astype(BF16)), mconv_s, sga_s, mod_s,
                               time_major(state_ffn[0]), w_bra_b, w_o_b, g_ffn, w_up_b, fw, fb,
                               w_dn_b, g_fin, slab=bd)

    k_p =ktf.reshape(b, n_heads, HEAD_DIM, s).transpose(0, 3, 1, 2)[None]
    v_p = vtf.reshape(b, n_heads, HEAD_DIM, s).transpose(0, 3, 1, 2)[None]
    k_s = ktf_s.reshape(n_heads, HEAD_DIM, t_new, bd).transpose(3, 2, 0, 1)[None]
    v_s = vtf_s.reshape(n_heads, HEAD_DIM, t_new, bd).transpose(3, 2, 0, 1)[None]
    return (y_p, batch_major(y_s[0], t_new), k_p, v_p, cst_p[None], fst_p[None], k_s, v_s,
            batch_major(cst_s[0], 2)[None], batch_major(fst_s[0], 2)[None])
```

```python
import functools

import jax
import jax.numpy as jnp
from jax import lax
from jax.experimental import pallas as pl
from jax.experimental.pallas import tpu as pltpu

F32 = jnp.float32
BF16 = jnp.bfloat16

EPS = 1e-6
N_MOD = 6
HEAD_DIM = 64
HEADS_PER_LANE_TILE = 2
V7X_LANES = 128
V7X_SUBLANES = 8
V7X_VMEM_LIMIT = 56 * 1024 * 1024

ROW_TILE = 512
FFN_CHUNK = 256
FFN_WINDOWS = 2
Q_TILE = 512
UNIT_KEYS = 256
MASKED_LOGIT = -1e30
LOG2E = 1.4426950408889634
K_TILE = 128


def _resident(block_shape, index_map):
    return pl.BlockSpec(block_shape, index_map, pipeline_mode=pl.Buffered(1))


def _expand_rows(m, rows):
    r = m.shape[0]
    if r == 1 or r == rows:
        return m
    return jnp.concatenate([m] * (rows // r), axis=0)


def _rms_norm(x, g):
    ms = jnp.mean(x * x, axis=-1, keepdims=True)
    return x * lax.rsqrt(ms + EPS) * g


def _softplus(z):
    return jnp.maximum(z, 0.0) + jnp.log(1.0 + jnp.exp(-jnp.abs(z)))


def _split_bf16(x):
    hi = x.astype(BF16)
    lo = (x - hi.astype(F32)).astype(BF16)
    return hi, lo


def _dot(a, b):
    return jnp.dot(a, b, preferred_element_type=F32)


def _dot_nt(a, b):
    return lax.dot_general(a, b, (((1,), (1,)), ((), ())), preferred_element_type=F32)


def _inclusive_suffix_matrix(keys):
    j = jnp.arange(keys)[:, None]
    s = jnp.arange(keys)[None, :]
    return (j >= s).astype(BF16)


def _suffix_matrix(k_tile):
    j = jnp.arange(k_tile)[:, None]
    s = jnp.arange(k_tile)[None, :]
    u = (j >= s).astype(BF16)
    half = jnp.concatenate([u, jnp.ones((k_tile, k_tile), BF16)], axis=1)
    return jnp.concatenate([half, half], axis=0)


def _causal_dwconv(ext_ref, pad, slab, rows, w_ref, b_ref, col0, width):
    cols = pl.ds(col0, width)
    y = b_ref[:, cols] + w_ref[2:3, cols] * ext_ref[pad:pad + rows, :]
    y = y + w_ref[1:2, cols] * ext_ref[pad - slab:pad - slab + rows, :]
    y = y + w_ref[0:1, cols] * ext_ref[pad - 2 * slab:pad - 2 * slab + rows, :]
    return y


def _ada_kernel(c_ref, w_ref, b_ref, o_ref):
    c = c_ref[...]
    ca = (c * jax.nn.sigmoid(c)).astype(BF16)
    o_ref[...] = _dot(ca, w_ref[...]) + b_ref[...]


def _ada_call(c_all, w_ada, b_ada):
    n, d = c_all.shape
    dn = w_ada.shape[1]
    tn = d
    return pl.pallas_call(
        _ada_kernel,
        grid=(dn // tn,),
        in_specs=[pl.BlockSpec((n, d), lambda j: (0, 0)),
                  pl.BlockSpec((d, tn), lambda j: (0, j)),
                  pl.BlockSpec((1, tn), lambda j: (0, j))],
        out_specs=pl.BlockSpec((n, tn), lambda j: (0, j)),
        out_shape=jax.ShapeDtypeStruct((n, dn), F32),
        compiler_params=pltpu.CompilerParams(dimension_semantics=("arbitrary",),
                                             vmem_limit_bytes=V7X_VMEM_LIMIT),
        name="ada_mod",
    )(c_all, w_ada, b_ada)


def _mix_in_kernel(x_ref, mod_ref, hist_ref, g_ref, w_in_ref, wkvt_ref, cw_ref, cb_ref, wbc_ref,
                   ktf_ref, vtf_ref, qb_ref, kb_ref, vb_ref, mconv_ref, sga_ref, cst_ref,
                   ext_ref, *, slab, pad, d_conv, d_attn, d_model, rowmajor_kv, q_scale):
    t = pl.program_id(1)
    rows = x_ref.shape[0]

    @pl.when(t == 0)
    def _():
        ext_ref[0:pad, :] = jnp.zeros((pad, d_conv), F32)
        ext_ref[pad - 2 * slab:pad, :] = hist_ref[...]

    shift = _expand_rows(mod_ref[0], rows)
    scale = _expand_rows(mod_ref[1], rows)
    hm = _rms_norm(x_ref[...], g_ref[...]) * (1.0 + scale) + shift
    hb = hm.astype(BF16)

    def proj(c0, width):
        return _dot(hb, w_in_ref[:, c0:c0 + width])

    gate_b = proj(0, d_conv)
    gate_c = proj(d_conv, d_conv)
    u_in = proj(2 * d_conv, d_conv)
    ext_ref[pad:pad + rows, :] = gate_c * u_in
    y_conv = gate_b * _causal_dwconv(ext_ref, pad, slab, rows, cw_ref, cb_ref, 0, d_conv)
    cst_ref[...] = ext_ref[pad + rows - 2 * slab:pad + rows, :]
    ext_ref[0:pad, :] = ext_ref[rows:rows + pad, :]

    off = 3 * d_conv
    q = proj(off, d_attn)
    qb_ref[...] = (q * q_scale).astype(BF16)
    kvt = _dot_nt(wkvt_ref[...], hb)
    ktf_ref[...] = kvt[0:d_attn]
    vtf_ref[...] = kvt[d_attn:2 * d_attn]
    if rowmajor_kv:
        kb_ref[...] = proj(off + d_attn, d_attn).astype(BF16)
        vb_ref[...] = proj(off + 2 * d_attn, d_attn).astype(BF16)
    else:
        for j in range(rows // K_TILE):
            kb_ref[j] = kvt[0:d_attn, j * K_TILE:(j + 1) * K_TILE].astype(BF16)
            vb_ref[j] = kvt[d_attn:2 * d_attn, j * K_TILE:(j + 1) * K_TILE].astype(BF16)

    off = off + 3 * d_attn
    sel_conv = proj(off, d_model)
    br_conv = _dot(y_conv.astype(BF16), wbc_ref[...])
    mconv_ref[...] = (jax.nn.sigmoid(sel_conv) * br_conv).astype(BF16)
    sel_attn = proj(off + d_model, d_model)
    sga_ref[...] = jax.nn.sigmoid(sel_attn).astype(BF16)


def _mix_in_call(x, mod, hist, g, w_in, w_kvt, conv_w, conv_b, w_br_conv, *, slab, d_attn,
                 rowmajor_kv, q_scale):
    nb, rows_total, d_model = x.shape
    d_conv = conv_w.shape[1]
    d_in = w_in.shape[1]
    tm = min(ROW_TILE, rows_total)
    nt = rows_total // tm
    r = mod.shape[2]
    pad = max(V7X_SUBLANES, 2 * slab)

    def row_spec(width):
        return pl.BlockSpec((None, tm, width), lambda b, t: (b, t, 0))

    def out(width, dtype):
        return jax.ShapeDtypeStruct((nb, rows_total, width), dtype)

    col_spec = pl.BlockSpec((None, d_attn, tm), lambda b, t: (b, 0, t))
    col_out = jax.ShapeDtypeStruct((nb, d_attn, rows_total), F32)
    if rowmajor_kv:
        kv_spec, kv_out = row_spec(d_attn), out(d_attn, BF16)
    else:
        kv_spec = pl.BlockSpec((None, tm // K_TILE, d_attn, K_TILE), lambda b, t: (b, t, 0, 0))
        kv_out = jax.ShapeDtypeStruct((nb, rows_total // K_TILE, d_attn, K_TILE), BF16)

    kern = functools.partial(_mix_in_kernel, slab=slab, pad=pad, d_conv=d_conv,
                             d_attn=d_attn, d_model=d_model, rowmajor_kv=rowmajor_kv,
                             q_scale=q_scale)
    return pl.pallas_call(
        kern,
        grid=(nb, nt),
        in_specs=[row_spec(d_model),
                  pl.BlockSpec((None, N_MOD, r, d_model), lambda b, t: (b, 0, 0, 0)),
                  pl.BlockSpec((None, 2 * slab, d_conv), lambda b, t: (b, 0, 0)),
                  _resident((1, d_model), lambda b, t: (0, 0)),
                  _resident((d_model, d_in), lambda b, t: (0, 0)),
                  _resident((2 * d_attn, d_model), lambda b, t: (0, 0)),
                  _resident((3, d_conv), lambda b, t: (0, 0)),
                  _resident((1, d_conv), lambda b, t: (0, 0)),
                  _resident((d_conv, d_model), lambda b, t: (0, 0))],
        out_specs=[col_spec, col_spec, row_spec(d_attn), kv_spec, kv_spec,
                   row_spec(d_model), row_spec(d_model),
                   pl.BlockSpec((None, 2 * slab, d_conv), lambda b, t: (b, 0, 0))],
        out_shape=[col_out, col_out, out(d_attn, BF16), kv_out, kv_out,
                   out(d_model, BF16), out(d_model, BF16),
                   jax.ShapeDtypeStruct((nb, 2 * slab, d_conv), F32)],
        scratch_shapes=[pltpu.VMEM((pad + tm, d_conv), F32)],
        compiler_params=pltpu.CompilerParams(dimension_semantics=("arbitrary", "arbitrary"),
                                             vmem_limit_bytes=V7X_VMEM_LIMIT),
        name="mix_in",
    )(x, mod, hist, g, w_in, w_kvt, conv_w, conv_b, w_br_conv)


def _mix_out_kernel(x_ref, o_ref, mconv_ref, sga_ref, mod_ref, hist_ref, wba_ref, wo_ref,
                    gf_ref, wup_ref, fw_ref, fb_ref, wdn_ref, gfin_ref,
                    y_ref, fst_ref,
                    car_ref, exta_ref, extg_ref, x1_ref, hf_ref, h_ref,
                    *, slab, pad, d_ff):
    t = pl.program_id(1)
    rows = x_ref.shape[0]

    @pl.when(t == 0)
    def _():
        car_ref[...] = jnp.zeros(car_ref.shape, F32)
        car_ref[pad - 2 * slab:pad, :] = hist_ref[...]

    br_attn = _dot(o_ref[...], wba_ref[...])
    merged = mconv_ref[...].astype(F32) + sga_ref[...].astype(F32) * br_attn
    gate_m = _expand_rows(mod_ref[2], rows)
    x1 = x_ref[...] + gate_m * _dot(merged.astype(BF16), wo_ref[...])
    x1_ref[...] = x1

    shift = _expand_rows(mod_ref[3], rows)
    scale = _expand_rows(mod_ref[4], rows)
    hf_ref[...] = (_rms_norm(x1, gf_ref[...]) * (1.0 + scale) + shift).astype(BF16)

    for c in range(d_ff // FFN_CHUNK):
        ca = c * FFN_CHUNK
        cg = d_ff + ca
        hf = hf_ref[...]
        exta = exta_ref.at[c % FFN_WINDOWS]
        extg = extg_ref.at[c % FFN_WINDOWS]
        exta[0:pad, :] = car_ref[:, ca:ca + FFN_CHUNK]
        extg[0:pad, :] = car_ref[:, cg:cg + FFN_CHUNK]
        exta[pad:pad + rows, :] = _dot(hf, wup_ref[:, ca:ca + FFN_CHUNK])
        extg[pad:pad + rows, :] = _dot(hf, wup_ref[:, cg:cg + FFN_CHUNK])
        a = _causal_dwconv(exta, pad, slab, rows, fw_ref, fb_ref, ca, FFN_CHUNK)
        g = _causal_dwconv(extg, pad, slab, rows, fw_ref, fb_ref, cg, FFN_CHUNK)
        car_ref[:, ca:ca + FFN_CHUNK] = exta[rows:rows + pad, :]
        car_ref[:, cg:cg + FFN_CHUNK] = extg[rows:rows + pad, :]
        h_ref[:, ca:ca + FFN_CHUNK] = (g * jax.nn.sigmoid(g) * a).astype(BF16)

    fst_ref[...] = car_ref[pad - 2 * slab:pad, :]
    gate_f = _expand_rows(mod_ref[5], rows)
    x2 = x1_ref[...] + gate_f * _dot(h_ref[...], wdn_ref[...])
    y_ref[...] = _rms_norm(x2, gfin_ref[...])


def _mix_out_call(x, o_attn, mconv, sga, mod, hist, w_br_attn, w_o, g_ffn, w_up, ffn_w, ffn_b,
                  w_down, g_final, *, slab):
    nb, rows_total, d_model = x.shape
    d_attn = o_attn.shape[2]
    d_ff = w_down.shape[0]
    tm = min(ROW_TILE, rows_total)
    nt = rows_total // tm
    r = mod.shape[2]
    pad = max(V7X_SUBLANES, 2 * slab)

    def row_spec(width):
        return pl.BlockSpec((None, tm, width), lambda b, t: (b, t, 0))

    def const(shape):
        return _resident(shape, lambda b, t: (0,) * len(shape))

    kern = functools.partial(_mix_out_kernel, slab=slab, pad=pad, d_ff=d_ff)
    return pl.pallas_call(
        kern,
        grid=(nb, nt),
        in_specs=[row_spec(d_model), row_spec(d_attn), row_spec(d_model), row_spec(d_model),
                  pl.BlockSpec((None, N_MOD, r, d_model), lambda b, t: (b, 0, 0, 0)),
                  pl.BlockSpec((None, 2 * slab, 2 * d_ff), lambda b, t: (b, 0, 0)),
                  const((d_attn, d_model)), const((d_model, d_model)), const((1, d_model)),
                  const((d_model, 2 * d_ff)), const((3, 2 * d_ff)), const((1, 2 * d_ff)),
                  const((d_ff, d_model)), const((1, d_model))],
        out_specs=[row_spec(d_model),
                   pl.BlockSpec((None, 2 * slab, 2 * d_ff), lambda b, t: (b, 0, 0))],
        out_shape=[jax.ShapeDtypeStruct((nb, rows_total, d_model), F32),
                   jax.ShapeDtypeStruct((nb, 2 * slab, 2 * d_ff), F32)],
        scratch_shapes=[pltpu.VMEM((pad, 2 * d_ff), F32),
                        pltpu.VMEM((FFN_WINDOWS, pad + tm, FFN_CHUNK), F32),
                        pltpu.VMEM((FFN_WINDOWS, pad + tm, FFN_CHUNK), F32),
                        pltpu.VMEM((tm, d_model), F32),
                        pltpu.VMEM((tm, d_model), BF16),
                        pltpu.VMEM((tm, d_ff), BF16)],
        compiler_params=pltpu.CompilerParams(dimension_semantics=("arbitrary", "arbitrary"),
                                             vmem_limit_bytes=V7X_VMEM_LIMIT),
        name="mix_out_ffn",
    )(x, o_attn, mconv, sga, mod, hist, w_br_attn, w_o, g_ffn, w_up, ffn_w, ffn_b, w_down,
      g_final)


def _sb_block(qh, bias, kblk, vblk, u2, carry, mask, keys_on_lanes):
    z = (_dot(qh, kblk) if keys_on_lanes else _dot_nt(qh, kblk)) + bias
    nlk = _softplus(z)
    if mask is not None:
        nlk = jnp.where(mask, nlk, 0.0)
    hi, lo = _split_bf16(nlk)
    s2 = _dot(jnp.concatenate([hi, lo], axis=1), u2)
    w = jnp.exp(z - s2[:, :K_TILE] - carry)
    if mask is not None:
        w = jnp.where(mask, w, 0.0)
    w = w.astype(BF16)
    pv = _dot_nt(w, vblk) if keys_on_lanes else _dot(w, vblk)
    return pv, carry + s2[:, K_TILE:]


def _softplus2(z):
    return jnp.maximum(LOG2E * jnp.log(1.0 + jnp.exp2(jnp.minimum(z, 126.0))), z)


def _prompt_attention(hp, i, bias_ref, q_ref, k_ref, v_ref, u_ref, o_ref,
                      qh_ref, z_ref, s_ref, bm_ref, acc_ref, car_ref):
    tq = q_ref.shape[0]
    nd = tq // UNIT_KEYS
    blocks = UNIT_KEYS // K_TILE
    nunit = (i + 1) * nd
    heads = range(HEADS_PER_LANE_TILE)

    @pl.when(i == 0)
    def _():
        key = lax.broadcasted_iota(jnp.int32, (tq, UNIT_KEYS), 1)
        row = lax.broadcasted_iota(jnp.int32, (tq, UNIT_KEYS), 0)
        for h in heads:
            bias = jnp.full((tq, UNIT_KEYS), bias_ref[HEADS_PER_LANE_TILE * hp + h] * LOG2E, F32)
            for d in range(nd):
                bm_ref[h, nd - 1 - d] = jnp.where(key + d * UNIT_KEYS < row, bias, MASKED_LOGIT)
            bm_ref[h, nd] = bias

    q = q_ref[...]
    lane = lax.broadcasted_iota(jnp.int32, q.shape, 1)
    for h in heads:
        qh_ref[h] = jnp.where((lane >= HEAD_DIM) == (h == 1), q, jnp.zeros_like(q))
    acc_ref[...] = jnp.zeros(acc_ref.shape, F32)
    car_ref[...] = jnp.zeros(car_ref.shape, F32)

    def unit_keys(ref, n):
        first = (nunit - 1 - n) * blocks
        return jnp.concatenate([ref[first + j] for j in range(blocks)], axis=1)

    def logits(n):
        kk = unit_keys(k_ref, n)
        idx = min(n, nd) if isinstance(n, int) else jnp.minimum(n, nd)
        for h in heads:
            z_ref[n & 3, h] = _dot(qh_ref[h], kk) + bm_ref[h, idx]

    def suffix_sums(n):
        for h in heads:
            s_ref[n & 3, h] = _dot(_softplus2(z_ref[n & 3, h]).astype(BF16), u_ref[...])

    def weights_times_v(n):
        vv = unit_keys(v_ref, n)
        top = lax.broadcasted_iota(jnp.int32, vv.shape, 0) < HEAD_DIM
        zero = jnp.zeros_like(vv)
        ws = []
        for h in heads:
            car = car_ref[h]
            t = s_ref[n & 3, h] + jnp.concatenate([car] * blocks, axis=1)
            ws.append(jnp.exp2(z_ref[n & 3, h] - t).astype(BF16))
            car_ref[h] = jnp.broadcast_to(t[:, 0:1], car.shape)
        vd = jnp.concatenate([jnp.where(top, vv, zero), jnp.where(top, zero, vv)], axis=1)
        acc_ref[...] += _dot_nt(jnp.concatenate(ws, axis=1), vd)

    logits(0)
    logits(1)
    suffix_sums(0)

    def trip(n, _):
        weights_times_v(n)
        suffix_sums(n + 1)
        logits(n + 2)
        return 0

    lax.fori_loop(0, nunit - 2, trip, 0)
    weights_times_v(nunit - 2)
    suffix_sums(nunit - 1)
    weights_times_v(nunit - 1)
    o_ref[...] = acc_ref[...].astype(o_ref.dtype)


def _attn_kernel(pt_ref, bias_ref, q_ref, k_ref, v_ref, u_ref,
                 qs_ref, kn_ref, vn_ref, biasr_ref, hmask_ref, nmask_ref, sel_ref, u2_ref,
                 *rest, n_pages_step, steps_per_batch, q_tiles, head_pairs):
    page_refs = rest[:2 * n_pages_step]
    o_ref, os_ref = rest[2 * n_pages_step:2 * n_pages_step + 2]
    qh_ref, z_ref, s_ref, bm_ref, acc_ref, car_ref, accs_ref, cars_ref = rest[2 * n_pages_step + 2:]
    step = pl.program_id(0)
    i = lax.rem(step, q_tiles)
    hp = lax.rem(lax.div(step, q_tiles), head_pairs)
    st = lax.rem(step, steps_per_batch)

    hmask = hmask_ref[...]
    qbd = qs_ref[...] * hmask.astype(BF16)
    bias = biasr_ref[...]
    u2 = u2_ref[...]

    @pl.when(st == 0)
    def _():
        fill = jnp.zeros((K_TILE - kn_ref.shape[0], kn_ref.shape[1]), BF16)
        pv, car = _sb_block(qbd, bias, jnp.concatenate([kn_ref[...], fill], axis=0),
                            jnp.concatenate([vn_ref[...], fill], axis=0), u2,
                            jnp.zeros(cars_ref.shape, F32), nmask_ref[...] > 0.0, False)
        accs_ref[...] = pv
        cars_ref[...] = car

    _prompt_attention(hp, i, bias_ref, q_ref, k_ref, v_ref, u_ref, o_ref,
                      qh_ref, z_ref, s_ref, bm_ref, acc_ref, car_ref)

    zs = [_dot(qbd, page_refs[p][...].astype(BF16)) + bias for p in range(n_pages_step)]
    s2s = []
    for z in zs:
        hi, lo = _split_bf16(_softplus(z))
        s2s.append(_dot(jnp.concatenate([hi, lo], axis=1), u2))
    car = cars_ref[...]
    pv = None
    for p in range(n_pages_step):
        w = jnp.exp(zs[p] - s2s[p][:, :K_TILE] - car).astype(BF16)
        part = _dot_nt(w, page_refs[n_pages_step + p][...].astype(BF16))
        pv = part if pv is None else pv + part
        car = car + s2s[p][:, K_TILE:]
    accs_ref[...] += pv
    cars_ref[...] = car

    @pl.when(st == steps_per_batch - 1)
    def _():
        hi, lo = _split_bf16(accs_ref[...] * hmask)
        sel = sel_ref[...]
        os_ref[...] = (_dot(sel, hi) + _dot(sel, lo))[0:os_ref.shape[0], :]


def _attn_call(qb, kb, vb, sb_bias, u_unit, q_exp, k_new, v_new, cache_k, cache_v, page_table,
               bias_rows, hmask, nmask, sel, u2, t_new):
    b, s, d_attn = qb.shape
    tq = min(Q_TILE, s)
    nd = tq // UNIT_KEYS
    assert nd >= 2, "the three-stage sweep needs at least two units per query tile"
    q_tiles = s // tq
    head_pairs = d_attn // V7X_LANES
    steps = b * head_pairs * q_tiles
    bd, rows, _ = q_exp.shape
    n_pages = page_table.shape[1]
    page = cache_k.shape[2]
    pps, rem = divmod(bd * n_pages, steps)
    assert rem == 0 and n_pages % pps == 0, "page groups must tile the prompt grid"
    steps_per_batch = n_pages // pps
    unit = (HEADS_PER_LANE_TILE, tq, UNIT_KEYS)

    def prompt_tile(g, pt):
        return (g // (head_pairs * q_tiles), g % q_tiles, (g // q_tiles) % head_pairs)

    def prompt_kv(g, pt):
        return (g // (head_pairs * q_tiles), 0, (g // q_tiles) % head_pairs, 0)

    def sample_row(g, pt):
        return (g // steps_per_batch, 0, 0)

    def const(a):
        return _resident(a.shape, lambda g, pt: (0,) * a.ndim)

    def page_spec(p):
        def index(g, pt):
            newest_first = n_pages - 1 - ((g % steps_per_batch) * pps + p)
            return (pt[(g // steps_per_batch) * n_pages + newest_first], 0, 0)
        return pl.BlockSpec((None, d_attn, page), index)

    tile_spec = pl.BlockSpec((None, tq, V7X_LANES), prompt_tile)
    kv_spec = pl.BlockSpec((None, s // K_TILE, V7X_LANES, K_TILE), prompt_kv)
    new_spec = pl.BlockSpec((None, k_new.shape[1], d_attn), sample_row)
    kern = functools.partial(_attn_kernel, n_pages_step=pps, steps_per_batch=steps_per_batch,
                             q_tiles=q_tiles, head_pairs=head_pairs)
    return pl.pallas_call(
        kern,
        grid_spec=pltpu.PrefetchScalarGridSpec(
            num_scalar_prefetch=1,
            grid=(steps,),
            in_specs=[pl.BlockSpec(memory_space=pltpu.SMEM), tile_spec, kv_spec, kv_spec,
                      const(u_unit),
                      pl.BlockSpec((None, rows, d_attn), sample_row), new_spec, new_spec,
                      const(bias_rows), const(hmask), const(nmask), const(sel), const(u2)]
                     + [page_spec(p) for p in range(pps)] * 2,
            out_specs=[tile_spec, pl.BlockSpec((None, t_new, d_attn), sample_row)],
            scratch_shapes=[pltpu.VMEM((HEADS_PER_LANE_TILE, tq, V7X_LANES), BF16),
                            pltpu.VMEM((4,) + unit, F32),
                            pltpu.VMEM((4,) + unit, F32),
                            pltpu.VMEM((HEADS_PER_LANE_TILE, nd + 1, tq, UNIT_KEYS), F32),
                            pltpu.VMEM((tq, V7X_LANES), F32),
                            pltpu.VMEM((HEADS_PER_LANE_TILE, tq, K_TILE), F32),
                            pltpu.VMEM((rows, d_attn), F32),
                            pltpu.VMEM((rows, K_TILE), F32)]),
        out_shape=[jax.ShapeDtypeStruct((b, s, d_attn), BF16),
                   jax.ShapeDtypeStruct((bd, t_new, d_attn), F32)],
        compiler_params=pltpu.CompilerParams(dimension_semantics=("arbitrary",),
                                             vmem_limit_bytes=V7X_VMEM_LIMIT),
        name="sb_attn",
    )(page_table.reshape(-1), sb_bias, qb, kb, vb, u_unit, q_exp, k_new, v_new, bias_rows, hmask,
      nmask, sel, u2, *([cache_k] * pps), *([cache_v] * pps))


def kernel(x_prompt, x_sample, cache_k, cache_v, state_conv, state_ffn, page_table, c_prompt, c_sample, w_ada, b_ada, norm_mix_g, w_in, sb_bias, conv_w, conv_b, w_br_conv, w_br_attn, w_o, norm_ffn_g, w_up, ffn_conv_w, ffn_conv_b, w_down, norm_final_g):
    assert w_ada.shape[0] == 1, "single-layer step"
    b, s, d_model = x_prompt.shape
    bd, t_new, _ = x_sample.shape
    n_heads = sb_bias.shape[1]
    d_attn = n_heads * HEAD_DIM
    d_conv = conv_w.shape[2]
    d_ff2 = w_up.shape[2]
    page = cache_k.shape[2]
    assert page == K_TILE and d_attn % V7X_LANES == 0

    w_ada_b, w_in_b, w_brc_b, w_bra_b, w_o_b, w_up_b, w_dn_b = (
        w[0].astype(BF16) for w in (w_ada, w_in, w_br_conv, w_br_attn, w_o, w_up, w_down))
    g_mix, g_ffn, g_fin = norm_mix_g[0][None], norm_ffn_g[0][None], norm_final_g[None]
    cw, cb = conv_w[0], conv_b[0][None]
    fw, fb = ffn_conv_w[0], ffn_conv_b[0][None]
    bias = sb_bias[0]
    u2 = _suffix_matrix(K_TILE)

    mod = _ada_call(jnp.concatenate([c_prompt, c_sample], axis=0), w_ada_b, b_ada[0][None])
    mod_p = mod[:b].reshape(b, N_MOD, 1, d_model)
    mod_s = mod[b:].reshape(1, bd, N_MOD, d_model).transpose(0, 2, 1, 3)

    kv0 = 3 * d_conv + d_attn
    w_kvt = w_in[0][:, kv0:kv0 + 2 * d_attn].T.astype(BF16)

    ktf, vtf, qb, kb, vb, mconv, sga, cst_p = _mix_in_call(
        x_prompt, mod_p, jnp.zeros((b, 2, d_conv), F32), g_mix, w_in_b, w_kvt, cw, cb, w_brc_b,
        slab=1, d_attn=d_attn, rowmajor_kv=False, q_scale=HEAD_DIM ** -0.5 * LOG2E)

    def time_major(a):
        return a.transpose(1, 0, 2).reshape(1, a.shape[1] * bd, a.shape[2])

    def batch_major(a, steps):
        return a.reshape(steps, bd, a.shape[-1]).transpose(1, 0, 2)

    xs = time_major(x_sample)
    ktf_s, vtf_s, qb_s, kb_s, vb_s, mconv_s, sga_s, cst_s = _mix_in_call(
        xs, mod_s, time_major(state_conv[0]), g_mix, w_in_b, w_kvt, cw, cb, w_brc_b,
        slab=bd, d_attn=d_attn, rowmajor_kv=True, q_scale=HEAD_DIM ** -0.5)

    rows = n_heads * t_new
    q_exp = jnp.tile(batch_major(qb_s, t_new)[:, None], (1, n_heads, 1, 1)).reshape(bd, rows, d_attn)
    new_pad = 16 - t_new
    k_new = jnp.pad(batch_major(kb_s, t_new), ((0, 0), (0, new_pad), (0, 0)))
    v_new = jnp.pad(batch_major(vb_s, t_new), ((0, 0), (0, new_pad), (0, 0)))
    r_head = jnp.arange(rows) // t_new
    r_time = jnp.arange(rows) % t_new
    bias_rows = jnp.broadcast_to(bias[r_head][:, None], (rows, K_TILE)).astype(F32)
    hmask = (jnp.arange(d_attn)[None, :] // HEAD_DIM == r_head[:, None]).astype(F32)
    nmask = (jnp.arange(K_TILE)[None, :] < r_time[:, None]).astype(F32)
    sel = (jnp.arange(V7X_SUBLANES)[:, None] == r_time[None, :]).astype(BF16)

    def page_layout(cache):
        return cache[0].transpose(0, 2, 3, 1).reshape(-1, d_attn, page)

    o_p, o_s = _attn_call(qb, kb, vb, bias, _inclusive_suffix_matrix(UNIT_KEYS), q_exp, k_new, v_new,
                          page_layout(cache_k), page_layout(cache_v), page_table, bias_rows, hmask,
                          nmask, sel, u2, t_new)
    y_p, fst_p = _mix_out_call(x_prompt, o_p, mconv, sga, mod_p, jnp.zeros((b, 2, d_ff2), F32),
                               w_bra_b, w_o_b, g_ffn, w_up_b, fw, fb, w_dn_b, g_fin, slab=1)
    y_s, fst_s = _mix_out_call(xs, time_major(o_s.astype(BF16)), mconv_s, sga_s, mod_s,
                               time_major(state_ffn[0]), w_bra_b, w_o_b, g_ffn, w_up_b, fw, fb,
                               w_dn_b, g_fin, slab=bd)

    k_p = ktf.reshape(b, n_heads, HEAD_DIM, s).transpose(0, 3, 1, 2)[None]
    v_p = vtf.reshape(b, n_heads, HEAD_DIM, s).transpose(0, 3, 1, 2)[None]
    k_s = ktf_s.reshape(n_heads, HEAD_DIM, t_new, bd).transpose(3, 2, 0, 1)[None]
    v_s = vtf_s.reshape(n_heads, HEAD_DIM, t_new, bd).transpose(3, 2, 0, 1)[None]
    return (y_p, batch_major(y_s[0], t_new), k_p, v_p, cst_p[None], fst_p[None], k_s, v_s,
            batch_major(cst_s[0], 2)[None], batch_major(fst_s[0], 2)[None])
```

```python
import functools

import jax
import jax.numpy as jnp
from jax import lax
from jax.experimental import pallas as pl
from jax.experimental.pallas import tpu as pltpu

F32 = jnp.float32
BF16 = jnp.bfloat16

EPS = 1e-6
N_MOD = 6
HEAD_DIM = 64
HEADS_PER_LANE_TILE = 2
V7X_LANES = 128
V7X_SUBLANES = 8
V7X_VMEM_LIMIT = 56 * 1024 * 1024

ROW_TILE = 512
FFN_CHUNK = 256
FFN_WINDOWS = 2
Q_TILE = 512
UNIT_KEYS = 256
MASKED_LOGIT = -1e30
LOG2E = 1.4426950408889634
K_TILE = 128


def _resident(block_shape, index_map):
    return pl.BlockSpec(block_shape, index_map, pipeline_mode=pl.Buffered(1))


def _expand_rows(m, rows):
    r = m.shape[0]
    if r == 1 or r == rows:
        return m
    return jnp.concatenate([m] * (rows // r), axis=0)


def _rms_norm(x, g):
    ms = jnp.mean(x * x, axis=-1, keepdims=True)
    return x * lax.rsqrt(ms + EPS) * g


def _softplus(z):
    return jnp.maximum(z, 0.0) + jnp.log(1.0 + jnp.exp(-jnp.abs(z)))


def _split_bf16(x):
    hi = x.astype(BF16)
    lo = (x - hi.astype(F32)).astype(BF16)
    return hi, lo


def _dot(a, b):
    return jnp.dot(a, b, preferred_element_type=F32)


def _dot_nt(a, b):
    return lax.dot_general(a, b, (((1,), (1,)), ((), ())), preferred_element_type=F32)


def _inclusive_suffix_matrix(keys):
    j = jnp.arange(keys)[:, None]
    s = jnp.arange(keys)[None, :]
    return (j >= s).astype(BF16)


def _suffix_matrix(k_tile):
    j = jnp.arange(k_tile)[:, None]
    s = jnp.arange(k_tile)[None, :]
    u = (j >= s).astype(BF16)
    half = jnp.concatenate([u, jnp.ones((k_tile, k_tile), BF16)], axis=1)
    return jnp.concatenate([half, half], axis=0)


def _causal_dwconv(ext_ref, pad, slab, rows, w_ref, b_ref, col0, width):
    cols = pl.ds(col0, width)
    y = b_ref[:, cols] + w_ref[2:3, cols] * ext_ref[pad:pad + rows, :]
    y = y + w_ref[1:2, cols] * ext_ref[pad - slab:pad - slab + rows, :]
    y = y + w_ref[0:1, cols] * ext_ref[pad - 2 * slab:pad - 2 * slab + rows, :]
    return y


def _ada_kernel(c_ref, w_ref, b_ref, o_ref):
    c = c_ref[...]
    ca = (c * jax.nn.sigmoid(c)).astype(BF16)
    o_ref[...] = _dot(ca, w_ref[...]) + b_ref[...]


def _ada_call(c_all, w_ada, b_ada):
    n, d = c_all.shape
    dn = w_ada.shape[1]
    tn = d
    return pl.pallas_call(
        _ada_kernel,
        grid=(dn // tn,),
        in_specs=[pl.BlockSpec((n, d), lambda j: (0, 0)),
                  pl.BlockSpec((d, tn), lambda j: (0, j)),
                  pl.BlockSpec((1, tn), lambda j: (0, j))],
        out_specs=pl.BlockSpec((n, tn), lambda j: (0, j)),
        out_shape=jax.ShapeDtypeStruct((n, dn), F32),
        compiler_params=pltpu.CompilerParams(dimension_semantics=("arbitrary",),
                                             vmem_limit_bytes=V7X_VMEM_LIMIT),
        name="ada_mod",
    )(c_all, w_ada, b_ada)


def _mix_in_kernel(x_ref, mod_ref, hist_ref, g_ref, w_in_ref, wkvt_ref, cw_ref, cb_ref, wbc_ref,
                   ktf_ref, vtf_ref, qb_ref, kb_ref, vb_ref, mconv_ref, sga_ref, cst_ref,
                   ext_ref, *, slab, pad, d_conv, d_attn, d_model, rowmajor_kv, q_scale):
    t = pl.program_id(1)
    rows = x_ref.shape[0]

    @pl.when(t == 0)
    def _():
        ext_ref[0:pad, :] = jnp.zeros((pad, d_conv), F32)
        ext_ref[pad - 2 * slab:pad, :] = hist_ref[...]

    shift = _expand_rows(mod_ref[0], rows)
    scale = _expand_rows(mod_ref[1], rows)
    hm = _rms_norm(x_ref[...], g_ref[...]) * (1.0 + scale) + shift
    hb = hm.astype(BF16)

    def proj(c0, width):
        return _dot(hb, w_in_ref[:, c0:c0 + width])

    gate_b = proj(0, d_conv)
    gate_c = proj(d_conv, d_conv)
    u_in = proj(2 * d_conv, d_conv)
    ext_ref[pad:pad + rows, :] = gate_c * u_in
    y_conv = gate_b * _causal_dwconv(ext_ref, pad, slab, rows, cw_ref, cb_ref, 0, d_conv)
    cst_ref[...] = ext_ref[pad + rows - 2 * slab:pad + rows, :]
    ext_ref[0:pad, :] = ext_ref[rows:rows + pad, :]

    off = 3 * d_conv
    q = proj(off, d_attn)
    qb_ref[...] = (q * q_scale).astype(BF16)
    kvt = _dot_nt(wkvt_ref[...], hb)
    ktf_ref[...] = kvt[0:d_attn]
    vtf_ref[...] = kvt[d_attn:2 * d_attn]
    if rowmajor_kv:
        kb_ref[...] = proj(off + d_attn, d_attn).astype(BF16)
        vb_ref[...] = proj(off + 2 * d_attn, d_attn).astype(BF16)
    else:
        for j in range(rows // K_TILE):
            kb_ref[j] = kvt[0:d_attn, j * K_TILE:(j + 1) * K_TILE].astype(BF16)
            vb_ref[j] = kvt[d_attn:2 * d_attn, j * K_TILE:(j + 1) * K_TILE].astype(BF16)

    off = off + 3 * d_attn
    sel_conv = proj(off, d_model)
    br_conv = _dot(y_conv.astype(BF16), wbc_ref[...])
    mconv_ref[...] = (jax.nn.sigmoid(sel_conv) * br_conv).astype(BF16)
    sel_attn = proj(off + d_model, d_model)
    sga_ref[...] = jax.nn.sigmoid(sel_attn).astype(BF16)


def _mix_in_call(x, mod, hist, g, w_in, w_kvt, conv_w, conv_b, w_br_conv, *, slab, d_attn,
                 rowmajor_kv, q_scale):
    nb, rows_total, d_model = x.shape
    d_conv = conv_w.shape[1]
    d_in = w_in.shape[1]
    tm = min(ROW_TILE, rows_total)
    nt = rows_total // tm
    r = mod.shape[2]
    pad = max(V7X_SUBLANES, 2 * slab)

    def row_spec(width):
        return pl.BlockSpec((None, tm, width), lambda b, t: (b, t, 0))

    def out(width, dtype):
        return jax.ShapeDtypeStruct((nb, rows_total, width), dtype)

    col_spec = pl.BlockSpec((None, d_attn, tm), lambda b, t: (b, 0, t))
    col_out = jax.ShapeDtypeStruct((nb, d_attn, rows_total), F32)
    if rowmajor_kv:
        kv_spec, kv_out = row_spec(d_attn), out(d_attn, BF16)
    else:
        kv_spec = pl.BlockSpec((None, tm // K_TILE, d_attn, K_TILE), lambda b, t: (b, t, 0, 0))
        kv_out = jax.ShapeDtypeStruct((nb, rows_total // K_TILE, d_attn, K_TILE), BF16)

    kern = functools.partial(_mix_in_kernel, slab=slab, pad=pad, d_conv=d_conv,
                             d_attn=d_attn, d_model=d_model, rowmajor_kv=rowmajor_kv,
                             q_scale=q_scale)
    return pl.pallas_call(
        kern,
        grid=(nb, nt),
        in_specs=[row_spec(d_model),
                  pl.BlockSpec((None, N_MOD, r, d_model), lambda b, t: (b, 0, 0, 0)),
                  pl.BlockSpec((None, 2 * slab, d_conv), lambda b, t: (b, 0, 0)),
                  _resident((1, d_model), lambda b, t: (0, 0)),
                  _resident((d_model, d_in), lambda b, t: (0, 0)),
                  _resident((2 * d_attn, d_model), lambda b, t: (0, 0)),
                  _resident((3, d_conv), lambda b, t: (0, 0)),
                  _resident((1, d_conv), lambda b, t: (0, 0)),
                  _resident((d_conv, d_model), lambda b, t: (0, 0))],
        out_specs=[col_spec, col_spec, row_spec(d_attn), kv_spec, kv_spec,
                   row_spec(d_model), row_spec(d_model),
                   pl.BlockSpec((None, 2 * slab, d_conv), lambda b, t: (b, 0, 0))],
        out_shape=[col_out, col_out, out(d_attn, BF16), kv_out, kv_out,
                   out(d_model, BF16), out(d_model, BF16),
                   jax.ShapeDtypeStruct((nb, 2 * slab, d_conv), F32)],
        scratch_shapes=[pltpu.VMEM((pad + tm, d_conv), F32)],
        compiler_params=pltpu.CompilerParams(dimension_semantics=("arbitrary", "arbitrary"),
                                             vmem_limit_bytes=V7X_VMEM_LIMIT),
        name="mix_in",
    )(x, mod, hist, g, w_in, w_kvt, conv_w, conv_b, w_br_conv)


def _mix_out_kernel(x_ref, o_ref, mconv_ref, sga_ref, mod_ref, hist_ref, wba_ref, wo_ref,
                    gf_ref, wup_ref, fw_ref, fb_ref, wdn_ref, gfin_ref,
                    y_ref, fst_ref,
                    car_ref, exta_ref, extg_ref, x1_ref, hf_ref, h_ref,
                    *, slab, pad, d_ff):
    t = pl.program_id(1)
    rows = x_ref.shape[0]

    @pl.when(t == 0)
    def _():
        car_ref[...] = jnp.zeros(car_ref.shape, F32)
        car_ref[pad - 2 * slab:pad, :] = hist_ref[...]

    br_attn = _dot(o_ref[...], wba_ref[...])
    merged = mconv_ref[...].astype(F32) + sga_ref[...].astype(F32) * br_attn
    gate_m = _expand_rows(mod_ref[2], rows)
    x1 = x_ref[...] + gate_m * _dot(merged.astype(BF16), wo_ref[...])
    x1_ref[...] = x1

    shift = _expand_rows(mod_ref[3], rows)
    scale = _expand_rows(mod_ref[4], rows)
    hf_ref[...] = (_rms_norm(x1, gf_ref[...]) * (1.0 + scale) + shift).astype(BF16)

    for c in range(d_ff // FFN_CHUNK):
        ca = c * FFN_CHUNK
        cg = d_ff + ca
        hf = hf_ref[...]
        exta = exta_ref.at[c % FFN_WINDOWS]
        extg = extg_ref.at[c % FFN_WINDOWS]
        exta[0:pad, :] = car_ref[:, ca:ca + FFN_CHUNK]
        extg[0:pad, :] = car_ref[:, cg:cg + FFN_CHUNK]
        exta[pad:pad + rows, :] = _dot(hf, wup_ref[:, ca:ca + FFN_CHUNK])
        extg[pad:pad + rows, :] = _dot(hf, wup_ref[:, cg:cg + FFN_CHUNK])
        a = _causal_dwconv(exta, pad, slab, rows, fw_ref, fb_ref, ca, FFN_CHUNK)
        g = _causal_dwconv(extg, pad, slab, rows, fw_ref, fb_ref, cg, FFN_CHUNK)
        car_ref[:, ca:ca + FFN_CHUNK] = exta[rows:rows + pad, :]
        car_ref[:, cg:cg + FFN_CHUNK] = extg[rows:rows + pad, :]
        h_ref[:, ca:ca + FFN_CHUNK] = (g * jax.nn.sigmoid(g) * a).astype(BF16)

    fst_ref[...] = car_ref[pad - 2 * slab:pad, :]
    gate_f = _expand_rows(mod_ref[5], rows)
    x2 = x1_ref[...] + gate_f * _dot(h_ref[...], wdn_ref[...])
    y_ref[...] = _rms_norm(x2, gfin_ref[...])


def _mix_out_call(x, o_attn, mconv, sga, mod, hist, w_br_attn, w_o, g_ffn, w_up, ffn_w, ffn_b,
                  w_down, g_final, *, slab):
    nb, rows_total, d_model = x.shape
    d_attn = o_attn.shape[2]
    d_ff = w_down.shape[0]
    tm = min(ROW_TILE, rows_total)
    nt = rows_total // tm
    r = mod.shape[2]
    pad = max(V7X_SUBLANES, 2 * slab)

    def row_spec(width):
        return pl.BlockSpec((None, tm, width), lambda b, t: (b, t, 0))

    def const(shape):
        return _resident(shape, lambda b, t: (0,) * len(shape))

    kern = functools.partial(_mix_out_kernel, slab=slab, pad=pad, d_ff=d_ff)
    return pl.pallas_call(
        kern,
        grid=(nb, nt),
        in_specs=[row_spec(d_model), row_spec(d_attn), row_spec(d_model), row_spec(d_model),
                  pl.BlockSpec((None, N_MOD, r, d_model), lambda b, t: (b, 0, 0, 0)),
                  pl.BlockSpec((None, 2 * slab, 2 * d_ff), lambda b, t: (b, 0, 0)),
                  const((d_attn, d_model)), const((d_model, d_model)), const((1, d_model)),
                  const((d_model, 2 * d_ff)), const((3, 2 * d_ff)), const((1, 2 * d_ff)),
                  const((d_ff, d_model)), const((1, d_model))],
        out_specs=[row_spec(d_model),
                   pl.BlockSpec((None, 2 * slab, 2 * d_ff), lambda b, t: (b, 0, 0))],
        out_shape=[jax.ShapeDtypeStruct((nb, rows_total, d_model), F32),
                   jax.ShapeDtypeStruct((nb, 2 * slab, 2 * d_ff), F32)],
        scratch_shapes=[pltpu.VMEM((pad, 2 * d_ff), F32),
                        pltpu.VMEM((FFN_WINDOWS, pad + tm, FFN_CHUNK), F32),
                        pltpu.VMEM((FFN_WINDOWS, pad + tm, FFN_CHUNK), F32),
                        pltpu.VMEM((tm, d_model), F32),
                        pltpu.VMEM((tm, d_model), BF16),
                        pltpu.VMEM((tm, d_ff), BF16)],
        compiler_params=pltpu.CompilerParams(dimension_semantics=("arbitrary", "arbitrary"),
                                             vmem_limit_bytes=V7X_VMEM_LIMIT),
        name="mix_out_ffn",
    )(x, o_attn, mconv, sga, mod, hist, w_br_attn, w_o, g_ffn, w_up, ffn_w, ffn_b, w_down,
      g_final)


def _sb_block(qh, bias, kblk, vblk, u2, carry, mask, keys_on_lanes):
    z = (_dot(qh, kblk) if keys_on_lanes else _dot_nt(qh, kblk)) + bias
    nlk = _softplus(z)
    if mask is not None:
        nlk = jnp.where(mask, nlk, 0.0)
    hi, lo = _split_bf16(nlk)
    s2 = _dot(jnp.concatenate([hi, lo], axis=1), u2)
    w = jnp.exp(z - s2[:, :K_TILE] - carry)
    if mask is not None:
        w = jnp.where(mask, w, 0.0)
    w = w.astype(BF16)
    pv = _dot_nt(w, vblk) if keys_on_lanes else _dot(w, vblk)
    return pv, carry + s2[:, K_TILE:]


def _softplus2(z):
    return jnp.maximum(LOG2E * jnp.log(1.0 + jnp.exp2(jnp.minimum(z, 126.0))), z)


def _prompt_attention(hp, i, bias_ref, q_ref, k_ref, v_ref, u_ref, o_ref,
                      qh_ref, z_ref, s_ref, bm_ref, acc_ref, car_ref):
    tq = q_ref.shape[0]
    nd = tq // UNIT_KEYS
    blocks = UNIT_KEYS // K_TILE
    nunit = (i + 1) * nd
    heads = range(HEADS_PER_LANE_TILE)

    @pl.when(i == 0)
    def _():
        key = lax.broadcasted_iota(jnp.int32, (tq, UNIT_KEYS), 1)
        row = lax.broadcasted_iota(jnp.int32, (tq, UNIT_KEYS), 0)
        for h in heads:
            bias = jnp.full((tq, UNIT_KEYS), bias_ref[HEADS_PER_LANE_TILE * hp + h] * LOG2E, F32)
            for d in range(nd):
                bm_ref[h, nd - 1 - d] = jnp.where(key + d * UNIT_KEYS < row, bias, MASKED_LOGIT)
            bm_ref[h, nd] = bias

    q = q_ref[...]
    lane = lax.broadcasted_iota(jnp.int32, q.shape, 1)
    for h in heads:
        qh_ref[h] = jnp.where((lane >= HEAD_DIM) == (h == 1), q, jnp.zeros_like(q))
    acc_ref[...] = jnp.zeros(acc_ref.shape, F32)
    car_ref[...] = jnp.zeros(car_ref.shape, F32)

    def unit_keys(ref, n):
        first = (nunit - 1 - n) * blocks
        return jnp.concatenate([ref[first + j] for j in range(blocks)], axis=1)

    def logits(n):
        kk = unit_keys(k_ref, n)
        idx = min(n, nd) if isinstance(n, int) else jnp.minimum(n, nd)
        for h in heads:
            z_ref[n & 3, h] = _dot(qh_ref[h], kk) + bm_ref[h, idx]

    def suffix_sums(n):
        for h in heads:
            s_ref[n & 3, h] = _dot(_softplus2(z_ref[n & 3, h]).astype(BF16), u_ref[...])

    def weights_times_v(n):
        vv = unit_keys(v_ref, n)
        top = lax.broadcasted_iota(jnp.int32, vv.shape, 0) < HEAD_DIM
        zero = jnp.zeros_like(vv)
        ws = []
        for h in heads:
            car = car_ref[h]
            t = s_ref[n & 3, h] + jnp.concatenate([car] * blocks, axis=1)
            ws.append(jnp.exp2(z_ref[n & 3, h] - t).astype(BF16))
            car_ref[h] = jnp.broadcast_to(t[:, 0:1], car.shape)
        vd = jnp.concatenate([jnp.where(top, vv, zero), jnp.where(top, zero, vv)], axis=1)
        acc_ref[...] += _dot_nt(jnp.concatenate(ws, axis=1), vd)

    logits(0)
    logits(1)
    suffix_sums(0)

    def trip(n, _):
        weights_times_v(n)
        suffix_sums(n + 1)
        logits(n + 2)
        return 0

    lax.fori_loop(0, nunit - 2, trip, 0)
    weights_times_v(nunit - 2)
    suffix_sums(nunit - 1)
    weights_times_v(nunit - 1)
    o_ref[...] = acc_ref[...].astype(o_ref.dtype)


def _page_copies(pt_ref, ck_ref, cv_ref, kpg_ref, vpg_ref, sem, step, slot, n_pages_step):
    copies = []
    for p in range(n_pages_step):
        phys = pt_ref[step * n_pages_step + p]
        copies.append(pltpu.make_async_copy(ck_ref.at[phys], kpg_ref.at[slot, p], sem.at[slot]))
        copies.append(pltpu.make_async_copy(cv_ref.at[phys], vpg_ref.at[slot, p], sem.at[slot]))
    return copies


def _attn_kernel(pt_ref, bias_ref, q_ref, k_ref, v_ref, u_ref,
                 qs_ref, kn_ref, vn_ref, biasr_ref, hmask_ref, nmask_ref, sel_ref, u2_ref,
                 ck_ref, cv_ref, o_ref, os_ref,
                 qh_ref, z_ref, s_ref, bm_ref, acc_ref, car_ref, accs_ref, cars_ref,
                 kpg_ref, vpg_ref, page_sem,
                 *, n_pages_step, steps_per_batch, q_tiles, head_pairs):
    step = pl.program_id(0)
    i = lax.rem(step, q_tiles)
    hp = lax.rem(lax.div(step, q_tiles), head_pairs)
    st = lax.rem(step, steps_per_batch)
    slot = lax.rem(step, 2)
    pages = functools.partial(_page_copies, pt_ref, ck_ref, cv_ref, kpg_ref, vpg_ref, page_sem,
                              n_pages_step=n_pages_step)

    @pl.when(step == 0)
    def _():
        for copy in pages(step=0, slot=0):
            copy.start()

    @pl.when(step + 1 < pl.num_programs(0))
    def _():
        for copy in pages(step=step + 1, slot=1 - slot):
            copy.start()

    hmask = hmask_ref[...]
    qbd = qs_ref[...] * hmask.astype(BF16)
    bias = biasr_ref[...]
    u2 = u2_ref[...]

    @pl.when(st == 0)
    def _():
        fill = jnp.zeros((K_TILE - kn_ref.shape[0], kn_ref.shape[1]), BF16)
        pv, car = _sb_block(qbd, bias, jnp.concatenate([kn_ref[...], fill], axis=0),
                            jnp.concatenate([vn_ref[...], fill], axis=0), u2,
                            jnp.zeros(cars_ref.shape, F32), nmask_ref[...] > 0.0, False)
        accs_ref[...] = pv
        cars_ref[...] = car

    _prompt_attention(hp, i, bias_ref, q_ref, k_ref, v_ref, u_ref, o_ref,
                      qh_ref, z_ref, s_ref, bm_ref, acc_ref, car_ref)

    for copy in pages(step=step, slot=slot):
        copy.wait()

    zs = [_dot(qbd, kpg_ref[slot, p].astype(BF16)) + bias for p in range(n_pages_step)]
    s2s = []
    for z in zs:
        hi, lo = _split_bf16(_softplus(z))
        s2s.append(_dot(jnp.concatenate([hi, lo], axis=1), u2))
    car = cars_ref[...]
    pv = None
    for p in range(n_pages_step):
        w = jnp.exp(zs[p] - s2s[p][:, :K_TILE] - car).astype(BF16)
        part = _dot_nt(w, vpg_ref[slot, p].astype(BF16))
        pv = part if pv is None else pv + part
        car = car + s2s[p][:, K_TILE:]
    accs_ref[...] += pv
    cars_ref[...] = car

    @pl.when(st == steps_per_batch - 1)
    def _():
        hi, lo = _split_bf16(accs_ref[...] * hmask)
        sel = sel_ref[...]
        os_ref[...] = (_dot(sel, hi) + _dot(sel, lo))[0:os_ref.shape[0], :]


def _attn_call(qb, kb, vb, sb_bias, u_unit, q_exp, k_new, v_new, cache_k, cache_v, page_table,
               bias_rows, hmask, nmask, sel, u2, t_new):
    b, s, d_attn = qb.shape
    tq = min(Q_TILE, s)
    nd = tq // UNIT_KEYS
    assert nd >= 2, "the three-stage sweep needs at least two units per query tile"
    q_tiles = s // tq
    head_pairs = d_attn // V7X_LANES
    steps = b * head_pairs * q_tiles
    bd, rows, _ = q_exp.shape
    n_pages = page_table.shape[1]
    page = cache_k.shape[2]
    pps, rem = divmod(bd * n_pages, steps)
    assert rem == 0 and n_pages % pps == 0, "page groups must tile the prompt grid"
    steps_per_batch = n_pages // pps
    unit = (HEADS_PER_LANE_TILE, tq, UNIT_KEYS)

    def prompt_tile(g, pt):
        return (g // (head_pairs * q_tiles), g % q_tiles, (g // q_tiles) % head_pairs)

    def prompt_kv(g, pt):
        return (g // (head_pairs * q_tiles), 0, (g // q_tiles) % head_pairs, 0)

    def sample_row(g, pt):
        return (g // steps_per_batch, 0, 0)

    def const(a):
        return _resident(a.shape, lambda g, pt: (0,) * a.ndim)

    tile_spec = pl.BlockSpec((None, tq, V7X_LANES), prompt_tile)
    kv_spec = pl.BlockSpec((None, s // K_TILE, V7X_LANES, K_TILE), prompt_kv)
    new_spec = pl.BlockSpec((None, k_new.shape[1], d_attn), sample_row)
    kern = functools.partial(_attn_kernel, n_pages_step=pps, steps_per_batch=steps_per_batch,
                             q_tiles=q_tiles, head_pairs=head_pairs)
    return pl.pallas_call(
        kern,
        grid_spec=pltpu.PrefetchScalarGridSpec(
            num_scalar_prefetch=1,
            grid=(steps,),
            in_specs=[pl.BlockSpec(memory_space=pltpu.SMEM), tile_spec, kv_spec, kv_spec,
                      const(u_unit),
                      pl.BlockSpec((None, rows, d_attn), sample_row), new_spec, new_spec,
                      const(bias_rows), const(hmask), const(nmask), const(sel), const(u2),
                      pl.BlockSpec(memory_space=pl.ANY), pl.BlockSpec(memory_space=pl.ANY)],
            out_specs=[tile_spec, pl.BlockSpec((None, t_new, d_attn), sample_row)],
            scratch_shapes=[pltpu.VMEM((HEADS_PER_LANE_TILE, tq, V7X_LANES), BF16),
                            pltpu.VMEM((4,) + unit, F32),
                            pltpu.VMEM((4,) + unit, F32),
                            pltpu.VMEM((HEADS_PER_LANE_TILE, nd + 1, tq, UNIT_KEYS), F32),
                            pltpu.VMEM((tq, V7X_LANES), F32),
                            pltpu.VMEM((HEADS_PER_LANE_TILE, tq, K_TILE), F32),
                            pltpu.VMEM((rows, d_attn), F32),
                            pltpu.VMEM((rows, K_TILE), F32),
                            pltpu.VMEM((2, pps, d_attn, page), F32),
                            pltpu.VMEM((2, pps, d_attn, page), F32),
                            pltpu.SemaphoreType.DMA((2,))]),
        out_shape=[jax.ShapeDtypeStruct((b, s, d_attn), BF16),
                   jax.ShapeDtypeStruct((bd, t_new, d_attn), F32)],
        compiler_params=pltpu.CompilerParams(dimension_semantics=("arbitrary",),
                                             vmem_limit_bytes=V7X_VMEM_LIMIT),
        name="sb_attn",
    )(page_table[:, ::-1].reshape(-1), sb_bias, qb, kb, vb, u_unit, q_exp, k_new, v_new, bias_rows, hmask,
      nmask, sel, u2, cache_k, cache_v)


def kernel(x_prompt, x_sample, cache_k, cache_v, state_conv, state_ffn, page_table, c_prompt, c_sample, w_ada, b_ada, norm_mix_g, w_in, sb_bias, conv_w, conv_b, w_br_conv, w_br_attn, w_o, norm_ffn_g, w_up, ffn_conv_w, ffn_conv_b, w_down, norm_final_g):
    assert w_ada.shape[0] == 1, "single-layer step"
    b, s, d_model = x_prompt.shape
    bd, t_new, _ = x_sample.shape
    n_heads = sb_bias.shape[1]
    d_attn = n_heads * HEAD_DIM
    d_conv = conv_w.shape[2]
    d_ff2 = w_up.shape[2]
    page = cache_k.shape[2]
    assert page == K_TILE and d_attn % V7X_LANES == 0

    w_ada_b, w_in_b, w_brc_b, w_bra_b, w_o_b, w_up_b, w_dn_b = (
        w[0].astype(BF16) for w in (w_ada, w_in, w_br_conv, w_br_attn, w_o, w_up, w_down))
    g_mix, g_ffn, g_fin = norm_mix_g[0][None], norm_ffn_g[0][None], norm_final_g[None]
    cw, cb = conv_w[0], conv_b[0][None]
    fw, fb = ffn_conv_w[0], ffn_conv_b[0][None]
    bias = sb_bias[0]
    u2 = _suffix_matrix(K_TILE)

    mod = _ada_call(jnp.concatenate([c_prompt, c_sample], axis=0), w_ada_b, b_ada[0][None])
    mod_p = mod[:b].reshape(b, N_MOD, 1, d_model)
    mod_s = mod[b:].reshape(1, bd, N_MOD, d_model).transpose(0, 2, 1, 3)

    kv0 = 3 * d_conv + d_attn
    w_kvt = w_in[0][:, kv0:kv0 + 2 * d_attn].T.astype(BF16)

    ktf, vtf, qb, kb, vb, mconv, sga, cst_p = _mix_in_call(
        x_prompt, mod_p, jnp.zeros((b, 2, d_conv), F32), g_mix, w_in_b, w_kvt, cw, cb, w_brc_b,
        slab=1, d_attn=d_attn, rowmajor_kv=False, q_scale=HEAD_DIM ** -0.5 * LOG2E)

    def time_major(a):
        return a.transpose(1, 0, 2).reshape(1, a.shape[1] * bd, a.shape[2])

    def batch_major(a, steps):
        return a.reshape(steps, bd, a.shape[-1]).transpose(1, 0, 2)

    xs = time_major(x_sample)
    ktf_s, vtf_s, qb_s, kb_s, vb_s, mconv_s, sga_s, cst_s = _mix_in_call(
        xs, mod_s, time_major(state_conv[0]), g_mix, w_in_b, w_kvt, cw, cb, w_brc_b,
        slab=bd, d_attn=d_attn, rowmajor_kv=True, q_scale=HEAD_DIM ** -0.5)

    rows = n_heads * t_new
    q_exp = jnp.tile(batch_major(qb_s, t_new)[:, None], (1, n_heads, 1, 1)).reshape(bd, rows, d_attn)
    new_pad = 16 - t_new
    k_new = jnp.pad(batch_major(kb_s, t_new), ((0, 0), (0, new_pad), (0, 0)))
    v_new = jnp.pad(batch_major(vb_s, t_new), ((0, 0), (0, new_pad), (0, 0)))
    r_head = jnp.arange(rows) // t_new
    r_time = jnp.arange(rows) % t_new
    bias_rows = jnp.broadcast_to(bias[r_head][:, None], (rows, K_TILE)).astype(F32)
    hmask = (jnp.arange(d_attn)[None, :] // HEAD_DIM == r_head[:, None]).astype(F32)
    nmask = (jnp.arange(K_TILE)[None, :] < r_time[:, None]).astype(F32)
    sel = (jnp.arange(V7X_SUBLANES)[:, None] == r_time[None, :]).astype(BF16)

    def page_layout(cache):
        return cache[0].transpose(0, 2, 3, 1).reshape(-1, d_attn, page)

    o_p, o_s = _attn_call(qb, kb, vb, bias, _inclusive_suffix_matrix(UNIT_KEYS), q_exp, k_new, v_new,
                          page_layout(cache_k), page_layout(cache_v), page_table, bias_rows, hmask,
                          nmask, sel, u2, t_new)
    y_p, fst_p = _mix_out_call(x_prompt, o_p, mconv, sga, mod_p, jnp.zeros((b, 2, d_ff2), F32),
                               w_bra_b, w_o_b, g_ffn, w_up_b, fw, fb, w_dn_b, g_fin, slab=1)
    y_s, fst_s = _mix_out_call(xs, time_major(o_s.astype(BF16)), mconv_s, sga_s, mod_s,
                               time_major(state_ffn[0]), w_bra_b, w_o_b, g_ffn, w_up_b, fw, fb,
                               w_dn_b, g_fin, slab=bd)

    k_p = ktf.reshape(b, n_heads, HEAD_DIM, s).transpose(0, 3, 1, 2)[None]
    v_p = vtf.reshape(b, n_heads, HEAD_DIM, s).transpose(0, 3, 1, 2)[None]
    k_s = ktf_s.reshape(n_heads, HEAD_DIM, t_new, bd).transpose(3, 2, 0, 1)[None]
    v_s = vtf_s.reshape(n_heads, HEAD_DIM, t_new, bd).transpose(3, 2, 0, 1)[None]
    return (y_p, batch_major(y_s[0], t_new), k_p, v_p, cst_p[None], fst_p[None], k_s, v_s,
            batch_major(cst_s[0], 2)[None], batch_major(fst_s[0], 2)[None])
```
